```python
import math
import jax, jax.numpy as jnp
from jax import lax
import numpy as np

D_MODEL = 1024
BATCH = 8
SEQ = 2048
DEPTH = 4

PLE_DIM = 256
SSM_WIDTH = 256
SSM_GROUP = 16
SSM_GROUPS = SSM_WIDTH // SSM_GROUP
SSM_STATE = 64
RET_HEADS = 6
RET_HEAD_DIM = 64
RET_WIDTH = RET_HEADS * RET_HEAD_DIM
RET_CHUNK = 128
DIFF_HEADS = 6
DIFF_QK_DIM = 32
DIFF_V_DIM = 64
DIFF_WIDTH = DIFF_HEADS * DIFF_V_DIM
ATTN_BLOCK = 128
MIX_WIDTH = SSM_WIDTH + RET_WIDTH + DIFF_WIDTH
PROJ_SIZES = (SSM_WIDTH,
              RET_WIDTH, RET_WIDTH, RET_WIDTH, RET_WIDTH,
              DIFF_HEADS * 2 * DIFF_QK_DIM,
              DIFF_HEADS * 2 * DIFF_QK_DIM,
              DIFF_WIDTH)
PROJ_WIDTH = sum(PROJ_SIZES)
PROJ_SPLITS = tuple(int(v) for v in np.cumsum(PROJ_SIZES)[:-1])
D_FF = 2816
CONV_WIDTH = 3
EPS = 1e-6

RET_LOG_GAMMA = np.log1p(-(2.0 ** (-5.0 - np.arange(RET_HEADS)))).astype(np.float32)
ALIBI_SLOPES = (2.0 ** (-8.0 * (np.arange(DIFF_HEADS) + 1) / DIFF_HEADS)).astype(np.float32)

kernel_name = "hybrid_s5_retention_diffattn_trunk"


def rmsnorm(x, g):
    xf = x.astype(jnp.float32)
    y = xf * lax.rsqrt(jnp.mean(xf * xf, axis=-1, keepdims=True) + EPS) * g.astype(jnp.float32)
    return y.astype(x.dtype)


def s5_mixer(u, lam_re, lam_im, log_dt, b_re, b_im, c_re, c_im, d_skip, w_glu, b_glu):
    f32 = jnp.float32
    bsz, L, _ = u.shape
    uf = u.astype(f32).reshape(bsz, L, SSM_GROUPS, SSM_GROUP)
    dt = jnp.exp(log_dt.astype(f32))[:, None]
    lr = lam_re.astype(f32)
    li = lam_im.astype(f32)
    mag = jnp.exp(lr * dt)
    ar = mag * jnp.cos(li * dt)
    ai = mag * jnp.sin(li * dt)
    den = lr * lr + li * li
    cr = ((ar - 1.0) * lr + ai * li) / den
    ci = (ai * lr - (ar - 1.0) * li) / den
    br = b_re.astype(f32)
    bi = b_im.astype(f32)
    bbr = cr[..., None] * br - ci[..., None] * bi
    bbi = cr[..., None] * bi + ci[..., None] * br
    bu_re = jnp.einsum('blgh,gph->blgp', uf, bbr)
    bu_im = jnp.einsum('blgh,gph->blgp', uf, bbi)
    a_re = jnp.broadcast_to(ar, bu_re.shape)
    a_im = jnp.broadcast_to(ai, bu_im.shape)

    def combine(e1, e2):
        a1r, a1i, b1r, b1i = e1
        a2r, a2i, b2r, b2i = e2
        return (a2r * a1r - a2i * a1i,
                a2r * a1i + a2i * a1r,
                a2r * b1r - a2i * b1i + b2r,
                a2r * b1i + a2i * b1r + b2i)

    _, _, xr, xi = lax.associative_scan(combine, (a_re, a_im, bu_re, bu_im), axis=1)
    y = (jnp.einsum('blgp,ghp->blgh', xr, c_re.astype(f32))
         - jnp.einsum('blgp,ghp->blgh', xi, c_im.astype(f32)))
    y = y.reshape(bsz, L, SSM_WIDTH) + d_skip.astype(f32) * uf.reshape(bsz, L, SSM_WIDTH)
    z = jax.nn.gelu(y)
    out = z * jax.nn.sigmoid(z @ w_glu.astype(f32) + b_glu.astype(f32))
    return out.astype(u.dtype)


def retention(q, k, v, g, gn_g):
    f32 = jnp.float32
    bsz, L, _ = q.shape
    H, dh, C = RET_HEADS, RET_HEAD_DIM, RET_CHUNK
    nc = L // C
    lg = jnp.asarray(RET_LOG_GAMMA)

    def chunks(t):
        return t.astype(f32).reshape(bsz, nc, C, H, dh).transpose(1, 0, 3, 2, 4)

    qc = chunks(q)
    kc = chunks(k) * (dh ** -0.5)
    vc = chunks(v)
    pos = jnp.arange(C, dtype=f32)
    dist = pos[:, None] - pos[None, :]
    decay = jnp.where(dist[None] >= 0,
                      jnp.exp(jnp.maximum(dist, 0.0)[None] * lg[:, None, None]), 0.0)
    inner = jnp.einsum('nbhcd,nbhsd->nbhcs', qc, kc) * decay
    inner_out = jnp.einsum('nbhcs,nbhse->nbhce', inner, vc)
    q_dec = qc * jnp.exp((pos + 1.0)[None, :] * lg[:, None])[:, :, None]
    k_dec = kc * jnp.exp((C - 1.0 - pos)[None, :] * lg[:, None])[:, :, None]
    chunk_decay = jnp.exp(C * lg)[None, :, None, None]

    def step(state, inp):
        qd, kd, vv = inp
        cross = jnp.einsum('bhcd,bhde->bhce', qd, state)
        state = state * chunk_decay + jnp.einsum('bhsd,bhse->bhde', kd, vv)
        return state, cross

    state0 = jnp.zeros((bsz, H, dh, dh), f32)
    _, cross = lax.scan(step, state0, (q_dec, k_dec, vc))
    o = (inner_out + cross).transpose(1, 0, 3, 2, 4).reshape(bsz, L, H, dh)
    mu = jnp.mean(o, axis=-1, keepdims=True)
    var = jnp.mean(jnp.square(o - mu), axis=-1, keepdims=True)
    o = ((o - mu) * lax.rsqrt(var + EPS)).reshape(bsz, L, RET_WIDTH) * gn_g.astype(f32)
    return (jax.nn.silu(g.astype(f32)) * o).astype(q.dtype)


def diff_attention(q, k, v, lq1, lk1, lq2, lk2, subln_g, lambda_init):
    f32 = jnp.float32
    bsz, L, _ = q.shape
    H, d, e = DIFF_HEADS, DIFF_QK_DIM, DIFF_V_DIM
    qf = q.astype(f32).reshape(bsz, L, H, 2, d) * (d ** -0.5)
    kf = k.astype(f32).reshape(bsz, L, H, 2, d)
    vf = v.astype(f32).reshape(bsz, L, H, e)
    lam = (jnp.exp(jnp.sum(lq1.astype(f32) * lk1.astype(f32)))
           - jnp.exp(jnp.sum(lq2.astype(f32) * lk2.astype(f32))) + lambda_init)
    slopes = jnp.asarray(ALIBI_SLOPES)
    outs = []
    for i in range(L // ATTN_BLOCK):
        q0 = i * ATTN_BLOCK
        kv_len = q0 + ATTN_BLOCK
        qb = qf[:, q0:kv_len]
        kb = kf[:, :kv_len]
        vb = vf[:, :kv_len]
        dist = (jnp.arange(q0, kv_len)[:, None] - jnp.arange(kv_len)[None, :]).astype(f32)
        bias = jnp.where(dist[None] >= 0, -slopes[:, None, None] * dist[None], -jnp.inf)
        s = jnp.einsum('bqhmd,bkhmd->bmhqk', qb, kb) + bias
        a = jax.nn.softmax(s, axis=-1)
        attn = a[:, 0] - lam * a[:, 1]
        outs.append(jnp.einsum('bhqk,bkhe->bqhe', attn, vb))
    o = jnp.concatenate(outs, axis=1)
    o = o * lax.rsqrt(jnp.mean(o * o, axis=-1, keepdims=True) + EPS)
    o = o * subln_g.astype(f32).reshape(H, e) * (1.0 - lambda_init)
    return o.reshape(bsz, L, DIFF_WIDTH).astype(q.dtype)


def conv_gated_mlp(x, w_up, conv_w, conv_b, w_down):
    L = x.shape[1]
    u = x @ w_up
    up = jnp.pad(u, ((0, 0), (CONV_WIDTH - 1, 0), (0, 0)))
    c = (conv_w[0] * up[:, 0:L] + conv_w[1] * up[:, 1:L + 1] + conv_w[2] * up[:, 2:L + 2] + conv_b)
    a, b = jnp.split(c, 2, axis=-1)
    return (jax.nn.silu(a) * b) @ w_down


def setup_inputs(seed: int = 0) -> dict:
    key = jax.random.key(seed)
    ks = jax.random.split(key, 32)
    f32 = jnp.float32
    nrm = lambda k, s: jax.random.normal(k, s, f32)
    gain = lambda k, s: 1.0 + 0.02 * nrm(k, s)
    x = nrm(ks[0], (BATCH, SEQ, D_MODEL))
    p = nrm(ks[1], (DEPTH, BATCH, SEQ, PLE_DIM))
    norm1_g = gain(ks[2], (DEPTH, D_MODEL))
    w_in = nrm(ks[3], (DEPTH, D_MODEL, PROJ_WIDTH)) * D_MODEL ** -0.5
    n_idx = jnp.arange(SSM_STATE, dtype=f32)
    ssm_lam_re = -0.5 + 0.01 * nrm(ks[4], (DEPTH, SSM_GROUPS, SSM_STATE))
    ssm_lam_im = math.pi * n_idx + 0.01 * nrm(ks[5], (DEPTH, SSM_GROUPS, SSM_STATE))
    ssm_log_dt = jax.random.uniform(ks[6], (DEPTH, SSM_GROUPS), f32, math.log(1e-3), math.log(1e-1))
    bsc = (0.5 / SSM_GROUP) ** 0.5
    csc = (0.5 / SSM_STATE) ** 0.5
    ssm_b_re = nrm(ks[7], (DEPTH, SSM_GROUPS, SSM_STATE, SSM_GROUP)) * bsc
    ssm_b_im = nrm(ks[8], (DEPTH, SSM_GROUPS, SSM_STATE, SSM_GROUP)) * bsc
    ssm_c_re = nrm(ks[9], (DEPTH, SSM_GROUPS, SSM_GROUP, SSM_STATE)) * csc
    ssm_c_im = nrm(ks[10], (DEPTH, SSM_GROUPS, SSM_GROUP, SSM_STATE)) * csc
    ssm_d = nrm(ks[11], (DEPTH, SSM_WIDTH))
    ssm_w_glu = nrm(ks[12], (DEPTH, SSM_WIDTH, SSM_WIDTH)) * SSM_WIDTH ** -0.5
    ssm_b_glu = 0.01 * nrm(ks[13], (DEPTH, SSM_WIDTH))
    ret_gn_g = gain(ks[14], (DEPTH, RET_WIDTH))
    diff_lq1 = 0.1 * nrm(ks[15], (DEPTH, DIFF_QK_DIM))
    diff_lk1 = 0.1 * nrm(ks[16], (DEPTH, DIFF_QK_DIM))
    diff_lq2 = 0.1 * nrm(ks[17], (DEPTH, DIFF_QK_DIM))
    diff_lk2 = 0.1 * nrm(ks[18], (DEPTH, DIFF_QK_DIM))
    diff_subln_g = gain(ks[19], (DEPTH, DIFF_WIDTH))
    w_out = nrm(ks[20], (DEPTH, MIX_WIDTH, D_MODEL)) * MIX_WIDTH ** -0.5
    norm2_g = gain(ks[21], (DEPTH, D_MODEL))
    w_up = nrm(ks[22], (DEPTH, D_MODEL, 2 * D_FF)) * D_MODEL ** -0.5
    conv_w = nrm(ks[23], (DEPTH, CONV_WIDTH, 2 * D_FF)) * CONV_WIDTH ** -0.5
    conv_b = 0.01 * nrm(ks[24], (DEPTH, 2 * D_FF))
    w_down = nrm(ks[25], (DEPTH, D_FF, D_MODEL)) * D_FF ** -0.5
    norm3_g = gain(ks[26], (DEPTH, D_MODEL))
    w_pg = nrm(ks[27], (DEPTH, D_MODEL, D_MODEL)) * D_MODEL ** -0.5
    w_pe = nrm(ks[28], (DEPTH, PLE_DIM, D_MODEL)) * PLE_DIM ** -0.5
    final_g = gain(ks[29], (D_MODEL,))
    return {"x": x, "p": p, "norm1_g": norm1_g, "w_in": w_in,
            "ssm_lam_re": ssm_lam_re, "ssm_lam_im": ssm_lam_im, "ssm_log_dt": ssm_log_dt,
            "ssm_b_re": ssm_b_re, "ssm_b_im": ssm_b_im, "ssm_c_re": ssm_c_re, "ssm_c_im": ssm_c_im,
            "ssm_d": ssm_d, "ssm_w_glu": ssm_w_glu, "ssm_b_glu": ssm_b_glu,
            "ret_gn_g": ret_gn_g,
            "diff_lq1": diff_lq1, "diff_lk1": diff_lk1, "diff_lq2": diff_lq2, "diff_lk2": diff_lk2,
            "diff_subln_g": diff_subln_g, "w_out": w_out,
            "norm2_g": norm2_g, "w_up": w_up, "conv_w": conv_w, "conv_b": conv_b, "w_down": w_down,
            "norm3_g": norm3_g, "w_pg": w_pg, "w_pe": w_pe, "final_g": final_g}


def reference(x, p, norm1_g, w_in, ssm_lam_re, ssm_lam_im, ssm_log_dt, ssm_b_re, ssm_b_im,
              ssm_c_re, ssm_c_im, ssm_d, ssm_w_glu, ssm_b_glu, ret_gn_g,
              diff_lq1, diff_lk1, diff_lq2, diff_lk2, diff_subln_g, w_out,
              norm2_g, w_up, conv_w, conv_b, w_down, norm3_g, w_pg, w_pe, final_g):
    h = x
    for l in range(DEPTH):
        lambda_init = 0.8 - 0.6 * math.exp(-0.3 * l)
        hn = rmsnorm(h, norm1_g[l])
        proj = hn @ w_in[l]
        s_u, r_q, r_k, r_v, r_g, d_q, d_k, d_v = jnp.split(proj, PROJ_SPLITS, axis=-1)
        s_out = s5_mixer(s_u, ssm_lam_re[l], ssm_lam_im[l], ssm_log_dt[l], ssm_b_re[l], ssm_b_im[l],
                         ssm_c_re[l], ssm_c_im[l], ssm_d[l], ssm_w_glu[l], ssm_b_glu[l])
        r_out = retention(r_q, r_k, r_v, r_g, ret_gn_g[l])
        d_out = diff_attention(d_q, d_k, d_v, diff_lq1[l], diff_lk1[l], diff_lq2[l], diff_lk2[l],
                               diff_subln_g[l], lambda_init)
        h = h + jnp.concatenate([s_out, r_out, d_out], axis=-1) @ w_out[l]
        h = h + conv_gated_mlp(rmsnorm(h, norm2_g[l]), w_up[l], conv_w[l], conv_b[l], w_down[l])
        gate = jax.nn.sigmoid(rmsnorm(h, norm3_g[l]) @ w_pg[l])
        h = h + gate * (p[l] @ w_pe[l])
    return rmsnorm(h, final_g)
```

```python
import functools
import math

import jax
import jax.numpy as jnp
import numpy as np
from jax import lax
from jax.experimental import pallas as pl
from jax.experimental.pallas import tpu as pltpu

F32 = jnp.float32
BF16 = jnp.bfloat16

D_MODEL = 1024
PLE_DIM = 256
SSM_WIDTH = 256
SSM_GROUP = 16
SSM_GROUPS = 16
SSM_STATE = 64
SSM_STATES = SSM_GROUPS * SSM_STATE
RET_HEADS = 6
RET_HEAD_DIM = 64
RET_WIDTH = RET_HEADS * RET_HEAD_DIM
DIFF_HEADS = 6
DIFF_QK_DIM = 32
DIFF_V_DIM = 64
DIFF_WIDTH = DIFF_HEADS * DIFF_V_DIM
D_FF = 2816
EPS = 1e-6
RET_LOG_GAMMA = np.log1p(-(2.0 ** (-5.0 - np.arange(RET_HEADS)))).astype(np.float32)
ALIBI_SLOPES = (2.0 ** (-8.0 * (np.arange(DIFF_HEADS) + 1) / DIFF_HEADS)).astype(np.float32)

LANES = 128
VMEM_BYTES_V7X = 64 * 1024 * 1024
NEG_BIG = -1e30

TM_PROJ = 512
TM_MLP = 256
FF_CHUNK = 256
S5_CHUNK = 128
RET_CHUNK = 256
ATT_TILE = 256
VT_TILE = 256


def _params(semantics, vmem_mib):
    assert vmem_mib * 1024 * 1024 < VMEM_BYTES_V7X
    return pltpu.CompilerParams(dimension_semantics=semantics, vmem_limit_bytes=vmem_mib * 1024 * 1024)


def _const_spec(shape):
    nd = len(shape)
    return pl.BlockSpec(shape, lambda *_: (0,) * nd)


def _rms(x, g):
    return x * lax.rsqrt(jnp.mean(x * x, axis=-1, keepdims=True) + EPS) * g


def _sigmoid(x):
    return 1.0 / (1.0 + jnp.exp(-x))


def _dot(a, b):
    return jnp.dot(a, b, preferred_element_type=F32)


def _dot_nt(a, b):
    return lax.dot_general(a, b, (((1,), (1,)), ((), ())), preferred_element_type=F32)


def _project(hn, w_u, w_ret, w_dq, w_dk, w_dvt, u_ref, ret_ref, dq_ref, dk_ref, vt_ref):
    u_ref[...] = _dot(hn, w_u[...])
    ret_ref[...] = _dot(hn, w_ret[...]).astype(BF16)
    dq_ref[...] = (_dot(hn, w_dq[...]) * (DIFF_QK_DIM ** -0.5)).astype(BF16)
    dk_ref[...] = _dot(hn, w_dk[...]).astype(BF16)
    vt = _dot_nt(w_dvt[...], hn).astype(BF16)
    for r in range(vt_ref.shape[0]):
        vt_ref[r] = vt[:, r * VT_TILE:(r + 1) * VT_TILE]


def _norm_proj_kernel(h_ref, g_ref, w_u, w_ret, w_dq, w_dk, w_dvt, u_ref, ret_ref, dq_ref, dk_ref, vt_ref):
    hn = _rms(h_ref[...], g_ref[...]).astype(BF16)
    _project(hn, w_u, w_ret, w_dq, w_dk, w_dvt, u_ref, ret_ref, dq_ref, dk_ref, vt_ref)


def _proj_out(n, tm):
    shapes = (
        jax.ShapeDtypeStruct((n, SSM_WIDTH), F32),
        jax.ShapeDtypeStruct((n, 4 * RET_WIDTH), BF16),
        jax.ShapeDtypeStruct((n, DIFF_WIDTH), BF16),
        jax.ShapeDtypeStruct((n, DIFF_WIDTH), BF16),
        jax.ShapeDtypeStruct((n // VT_TILE, DIFF_WIDTH, VT_TILE), BF16),
    )
    specs = (
        pl.BlockSpec((tm, SSM_WIDTH), lambda i: (i, 0)),
        pl.BlockSpec((tm, 4 * RET_WIDTH), lambda i: (i, 0)),
        pl.BlockSpec((tm, DIFF_WIDTH), lambda i: (i, 0)),
        pl.BlockSpec((tm, DIFF_WIDTH), lambda i: (i, 0)),
        pl.BlockSpec((tm // VT_TILE, DIFF_WIDTH, VT_TILE), lambda i: (i, 0, 0)),
    )
    return shapes, specs


def _proj_weight_specs(w):
    return [_const_spec(w[k].shape) for k in ("w_u", "w_ret", "w_dq", "w_dk", "w_dvt")]


def _proj_weights(w):
    return [w[k] for k in ("w_u", "w_ret", "w_dq", "w_dk", "w_dvt")]


def norm_proj(h, g, w):
    n = h.shape[0]
    tm = TM_PROJ
    shapes, specs = _proj_out(n, tm)
    return pl.pallas_call(
        _norm_proj_kernel,
        out_shape=shapes,
        grid=(n // tm,),
        in_specs=[pl.BlockSpec((tm, D_MODEL), lambda i: (i, 0)), _const_spec(g.shape)] + _proj_weight_specs(w),
        out_specs=specs,
        compiler_params=_params(("parallel",), 48),
        name="norm_proj",
    )(h, g, *_proj_weights(w))


def _s5_prep_kernel(lr_ref, li_ref, ldt_ref, brt_ref, bit_ref, crt_ref, cit_ref, b_ref, c_ref, a_ref):
    lr = lr_ref[0]
    li = li_ref[0]
    dt = jnp.exp(ldt_ref[0])
    mag = jnp.exp(lr * dt)
    ar = mag * jnp.cos(li * dt)
    ai = mag * jnp.sin(li * dt)
    den = lr * lr + li * li
    cr = ((ar - 1.0) * lr + ai * li) / den
    ci = (ai * lr - (ar - 1.0) * li) / den
    b_ref[...] = jnp.zeros_like(b_ref)
    c_ref[...] = jnp.zeros_like(c_ref)
    for g in range(SSM_GROUPS):
        rows = slice(g * SSM_GROUP, (g + 1) * SSM_GROUP)
        cols = slice(g * SSM_STATE, (g + 1) * SSM_STATE)
        cols_im = slice(SSM_STATES + g * SSM_STATE, SSM_STATES + (g + 1) * SSM_STATE)
        crg = cr[g:g + 1, :]
        cig = ci[g:g + 1, :]
        br = brt_ref[0, g]
        bi = bit_ref[0, g]
        b_ref[0, rows, cols] = crg * br - cig * bi
        b_ref[0, rows, cols_im] = crg * bi + cig * br
        c_ref[0, cols, rows] = crt_ref[0, g]
        c_ref[0, cols_im, rows] = -cit_ref[0, g]
        a_ref[0, 0:1, cols] = ar[g:g + 1, :]
        a_ref[0, 1:2, cols] = ai[g:g + 1, :]


def s5_prep(lam_re, lam_im, log_dt, b_re, b_im, c_re, c_im):
    depth = lam_re.shape[0]
    brt = jnp.transpose(b_re, (0, 1, 3, 2))
    bit = jnp.transpose(b_im, (0, 1, 3, 2))
    crt = jnp.transpose(c_re, (0, 1, 3, 2))
    cit = jnp.transpose(c_im, (0, 1, 3, 2))
    ldt = log_dt[..., None]

    def spec(a):
        nd = a.ndim
        return pl.BlockSpec((1,) + a.shape[1:], lambda l: (l,) + (0,) * (nd - 1))

    ins = (lam_re, lam_im, ldt, brt, bit, crt, cit)
    out_shape = (
        jax.ShapeDtypeStruct((depth, SSM_WIDTH, 2 * SSM_STATES), F32),
        jax.ShapeDtypeStruct((depth, 2 * SSM_STATES, SSM_WIDTH), F32),
        jax.ShapeDtypeStruct((depth, 2, SSM_STATES), F32),
    )
    return pl.pallas_call(
        _s5_prep_kernel,
        out_shape=out_shape,
        grid=(depth,),
        in_specs=[spec(a) for a in ins],
        out_specs=tuple(spec(o) for o in out_shape),
        compiler_params=_params(("parallel",), 32),
        name="s5_prep",
    )(*ins)


def _gelu_tanh(x):
    return 0.5 * x * (1.0 + jnp.tanh(math.sqrt(2.0 / math.pi) * (x + 0.044715 * (x * x * x))))


def _s5_kernel(u_ref, bmat_ref, cmat_ref, a_ref, d_ref, wglu_ref, bglu_ref, o_ref, x_scr, st_scr, *, tc):
    nb = u_ref.shape[0]

    @pl.when(pl.program_id(0) == 0)
    def _():
        st_scr[...] = jnp.zeros_like(st_scr)

    u_tb = pltpu.einshape("btd->tbd", u_ref[...]).reshape(tc * nb, SSM_WIDTH)
    x_scr[...] = _dot(u_tb.astype(BF16), bmat_ref[...])
    ar = jnp.broadcast_to(a_ref[0:1, :], (nb, SSM_STATES))
    ai = jnp.broadcast_to(a_ref[1:2, :], (nb, SSM_STATES))

    def step(t, carry):
        xr, xi = carry
        r = pl.multiple_of(t * nb, nb)
        nxr = ar * xr - ai * xi + x_scr[pl.ds(r, nb), 0:SSM_STATES]
        nxi = ar * xi + ai * xr + x_scr[pl.ds(r, nb), SSM_STATES:2 * SSM_STATES]
        x_scr[pl.ds(r, nb), 0:SSM_STATES] = nxr
        x_scr[pl.ds(r, nb), SSM_STATES:2 * SSM_STATES] = nxi
        return nxr, nxi

    xr, xi = lax.fori_loop(0, tc, step, (st_scr[:, 0:SSM_STATES], st_scr[:, SSM_STATES:2 * SSM_STATES]))
    st_scr[:, 0:SSM_STATES] = xr
    st_scr[:, SSM_STATES:2 * SSM_STATES] = xi

    y = _dot(x_scr[...].astype(BF16), cmat_ref[...]) + d_ref[...] * u_tb
    z = _gelu_tanh(y)
    out = z * _sigmoid(_dot(z.astype(BF16), wglu_ref[...]) + bglu_ref[...])
    o_ref[...] = pltpu.einshape("tbd->btd", out.reshape(tc, nb, SSM_WIDTH)).astype(BF16)


def s5_mixer(u, bmat, cmat, a, d, wglu, bglu):
    nb, seq, _ = u.shape
    tc = S5_CHUNK
    blk = pl.BlockSpec((nb, tc, SSM_WIDTH), lambda c: (0, c, 0))
    consts = (bmat, cmat, a, d, wglu, bglu)
    return pl.pallas_call(
        functools.partial(_s5_kernel, tc=tc),
        out_shape=jax.ShapeDtypeStruct(u.shape, BF16),
        grid=(seq // tc,),
        in_specs=[blk] + [_const_spec(c.shape) for c in consts],
        out_specs=blk,
        scratch_shapes=[pltpu.VMEM((tc * nb, 2 * SSM_STATES), F32), pltpu.VMEM((nb, 2 * SSM_STATES), F32)],
        compiler_params=_params(("arbitrary",), 48),
        name="s5_mixer",
    )(u, *consts)


def _ret_kernel(q_ref, k_ref, v_ref, g_ref, gn_ref, o_ref, dmat_scr, decq_scr, deck_scr, sdec_scr, st_scr, *, cr):
    pairs = RET_HEADS // 2
    hd = RET_HEAD_DIM
    scale = hd ** -0.5
    lane = lax.broadcasted_iota(jnp.int32, (1, LANES), 1)
    lo = lane < hd
    rows = lax.broadcasted_iota(jnp.int32, (LANES, LANES), 0) < hd
    cols = lax.broadcasted_iota(jnp.int32, (LANES, LANES), 1) < hd
    same = rows == cols

    @pl.when((pl.program_id(0) == 0) & (pl.program_id(1) == 0))
    def _():
        t = lax.broadcasted_iota(jnp.int32, (cr, cr), 0)
        s = lax.broadcasted_iota(jnp.int32, (cr, cr), 1)
        dist = (t - s).astype(F32)
        pos = lax.broadcasted_iota(jnp.int32, (cr, LANES), 0).astype(F32)
        for h in range(RET_HEADS):
            lg = float(RET_LOG_GAMMA[h])
            dmat_scr[h] = jnp.where(dist >= 0, jnp.exp(jnp.maximum(dist, 0.0) * lg), 0.0) * scale
        for j in range(pairs):
            lga, lgb = float(RET_LOG_GAMMA[2 * j]), float(RET_LOG_GAMMA[2 * j + 1])
            lg = jnp.where(lo, lga, lgb)
            decq_scr[j] = jnp.exp((pos + 1.0) * lg)
            deck_scr[j] = jnp.exp((cr - 1.0 - pos) * lg) * scale
            sdec_scr[j] = jnp.where(same, jnp.where(rows, math.exp(cr * lga), math.exp(cr * lgb)), 0.0)

    @pl.when(pl.program_id(1) == 0)
    def _():
        st_scr[...] = jnp.zeros_like(st_scr)

    avg = jnp.where(same, 1.0 / hd, 0.0).astype(F32)
    blockmask = jnp.where(same, 1.0, 0.0).astype(F32)
    for j in range(pairs):
        sl = slice(j * LANES, (j + 1) * LANES)
        q = q_ref[:, sl]
        k = k_ref[:, sl]
        v = v_ref[:, sl]
        zero = jnp.zeros_like(q)
        sa = _dot_nt(jnp.where(lo, q, zero), k)
        sb = _dot_nt(jnp.where(lo, zero, q), k)
        pa = (sa * dmat_scr[2 * j]).astype(BF16)
        pb = (sb * dmat_scr[2 * j + 1]).astype(BF16)
        inner = _dot(pa, jnp.where(lo, v, zero)) + _dot(pb, jnp.where(lo, zero, v))
        st = st_scr[j]
        cross = _dot((q.astype(F32) * decq_scr[j]).astype(BF16), st.astype(BF16))
        kdt = (k.astype(F32) * deck_scr[j]).T.astype(BF16)
        st_scr[j] = st * sdec_scr[j] + _dot(kdt, v) * blockmask
        o = inner + cross
        mu = jnp.dot(o, avg, precision=lax.Precision.HIGHEST, preferred_element_type=F32)
        dev = o - mu
        var = jnp.dot(dev * dev, avg, precision=lax.Precision.HIGHEST, preferred_element_type=F32)
        on = dev * lax.rsqrt(var + EPS) * gn_ref[:, sl]
        gate = g_ref[:, sl].astype(F32)
        o_ref[:, sl] = (gate * _sigmoid(gate) * on).astype(BF16)


def retention(ret, gn, nb, seq):
    n = ret.shape[0]
    cr = RET_CHUNK
    nc = seq // cr
    pairs = RET_HEADS // 2

    def part(col):
        return pl.BlockSpec((cr, RET_WIDTH), lambda b, c: (b * nc + c, col))

    return pl.pallas_call(
        functools.partial(_ret_kernel, cr=cr),
        out_shape=jax.ShapeDtypeStruct((n, RET_WIDTH), BF16),
        grid=(nb, nc),
        in_specs=[part(0), part(1), part(2), part(3), _const_spec(gn.shape)],
        out_specs=pl.BlockSpec((cr, RET_WIDTH), lambda b, c: (b * nc + c, 0)),
        scratch_shapes=[
            pltpu.VMEM((RET_HEADS, cr, cr), F32),
            pltpu.VMEM((pairs, cr, LANES), F32),
            pltpu.VMEM((pairs, cr, LANES), F32),
            pltpu.VMEM((pairs, LANES, LANES), F32),
            pltpu.VMEM((pairs, LANES, LANES), F32),
        ],
        compiler_params=_params(("arbitrary", "arbitrary"), 32),
        name="retention",
    )(ret, ret, ret, ret, gn)


def _diff_kernel(lq1_ref, lk1_ref, lq2_ref, lk2_ref, q_ref, k_ref, vt_ref, g_ref, o_ref,
                 acc_scr, m_scr, bias_scr, *, tile, lambda_init):
    j = pl.program_id(1)
    i = pl.program_id(2)
    hv = DIFF_V_DIM
    ones_rows = 16
    combos = 4

    lam = (jnp.exp(jnp.sum(lq1_ref[...] * lk1_ref[...], axis=-1, keepdims=True))
           - jnp.exp(jnp.sum(lq2_ref[...] * lk2_ref[...], axis=-1, keepdims=True)) + lambda_init)

    def pick(vals):
        return jnp.where(j == 0, vals[0], jnp.where(j == 1, vals[1], vals[2]))

    slopes = (pick([float(ALIBI_SLOPES[h]) for h in (0, 2, 4)]), pick([float(ALIBI_SLOPES[h]) for h in (1, 3, 5)]))
    s_loc = lax.broadcasted_iota(jnp.int32, (tile, tile), 0)
    t_loc = lax.broadcasted_iota(jnp.int32, (tile, tile), 1)
    for hl in range(2):
        bias_scr[hl] = s_loc.astype(F32) * slopes[hl]
    acc_scr[...] = jnp.zeros_like(acc_scr)
    m_scr[...] = jnp.full_like(m_scr, NEG_BIG)

    q = q_ref[...]
    lane = lax.broadcasted_iota(jnp.int32, (1, LANES), 1)
    zero = jnp.zeros_like(q)
    qm = [jnp.where((lane >= DIFF_QK_DIM * c) & (lane < DIFF_QK_DIM * (c + 1)), q, zero) for c in range(combos)]
    ones = jnp.ones((ones_rows, tile), BF16)

    def do_tile(kvt, diag):
        kv0 = pl.multiple_of(kvt * tile, tile)
        k_t = k_ref[pl.ds(kv0, tile), :]
        vt = vt_ref[kvt]
        off = ((kvt - i) * tile).astype(F32)
        for c in range(combos):
            hl = c // 2
            shift = slopes[hl] * off
            st = _dot_nt(k_t, qm[c]) + bias_scr[hl]
            if diag:
                st = jnp.where(s_loc <= t_loc, st, NEG_BIG)
            m_old = m_scr[c]
            m_new = jnp.maximum(m_old, jnp.max(st, axis=0, keepdims=True) + shift)
            p = jnp.exp(st - (m_new - shift))
            alpha = jnp.exp(m_old - m_new)
            vaug = jnp.concatenate([vt[hl * hv:(hl + 1) * hv, :], ones], axis=0)
            acc_scr[c] = acc_scr[c] * alpha + _dot(vaug, p.astype(BF16))
            m_scr[c] = m_new

    def body(kvt, carry):
        do_tile(kvt, False)
        return carry

    lax.fori_loop(0, i, body, 0)
    do_tile(i, True)

    outs = []
    for hl in range(2):
        a0 = acc_scr[2 * hl]
        a1 = acc_scr[2 * hl + 1]
        o = a0[0:hv] / a0[hv:hv + 1] - lam * (a1[0:hv] / a1[hv:hv + 1])
        outs.append(o * lax.rsqrt(jnp.mean(o * o, axis=0, keepdims=True) + EPS))
    ot = jnp.concatenate(outs, axis=0)
    o_ref[...] = (ot.T * g_ref[...] * (1.0 - lambda_init)).astype(BF16)


def diff_attention(dq, dk, vt, lq1, lk1, lq2, lk2, subln_g, lambda_init, nb, seq):
    n = dq.shape[0]
    tile = ATT_TILE
    nq = seq // tile
    pairs = DIFF_HEADS // 2
    lam_spec = _const_spec(lq1.shape)
    return pl.pallas_call(
        functools.partial(_diff_kernel, tile=tile, lambda_init=lambda_init),
        out_shape=jax.ShapeDtypeStruct((n, DIFF_WIDTH), BF16),
        grid=(nb, pairs, nq),
        in_specs=[
            lam_spec, lam_spec, lam_spec, lam_spec,
            pl.BlockSpec((tile, LANES), lambda b, j, i: (b * nq + i, j)),
            pl.BlockSpec((seq, LANES), lambda b, j, i: (b, j)),
            pl.BlockSpec((seq // VT_TILE, LANES, VT_TILE), lambda b, j, i: (b, j, 0)),
            pl.BlockSpec((1, LANES), lambda b, j, i: (0, j)),
        ],
        out_specs=pl.BlockSpec((tile, LANES), lambda b, j, i: (b * nq + i, j)),
        scratch_shapes=[
            pltpu.VMEM((4, DIFF_V_DIM + 16, tile), F32),
            pltpu.VMEM((4, 1, tile), F32),
            pltpu.VMEM((2, tile, tile), F32),
        ],
        compiler_params=_params(("parallel", "parallel", "arbitrary"), 32),
        name="diff_attention",
    )(lq1, lk1, lq2, lk2, dq, dk, vt, subln_g)


def _mlp_kernel(h_ref, s_ref, r_ref, d_ref, wos_ref, wor_ref, wod_ref, g2_ref, wup_ref, cw_ref, cb_ref, wdn_ref,
                o_ref, carry_scr, *, tm, tiles_per_seq):
    @pl.when(pl.program_id(0) % tiles_per_seq == 0)
    def _():
        carry_scr[...] = jnp.zeros_like(carry_scr)

    h1 = (h_ref[...] + _dot(s_ref[...], wos_ref[...]) + _dot(r_ref[...], wor_ref[...])
          + _dot(d_ref[...], wod_ref[...]))
    hn = _rms(h1, g2_ref[...]).astype(BF16)
    row8 = lax.broadcasted_iota(jnp.int32, (8, FF_CHUNK), 0)

    def conv_cols(c0):
        u = _dot(hn, wup_ref[:, c0:c0 + FF_CHUNK])
        c6 = carry_scr[6:7, c0:c0 + FF_CHUNK]
        c7 = carry_scr[7:8, c0:c0 + FF_CHUNK]
        r1 = pltpu.roll(u, 1, 0)
        r2 = pltpu.roll(u, 2, 0)
        p1 = jnp.concatenate([jnp.where(row8 == 0, c7, r1[0:8]), r1[8:]], axis=0)
        p2 = jnp.concatenate([jnp.where(row8 == 0, c6, jnp.where(row8 == 1, c7, r2[0:8])), r2[8:]], axis=0)
        carry_scr[:, c0:c0 + FF_CHUNK] = u[tm - 8:tm, :]
        w = cw_ref[:, c0:c0 + FF_CHUNK]
        return w[0:1] * p2 + w[1:2] * p1 + w[2:3] * u + cb_ref[:, c0:c0 + FF_CHUNK]

    acc = jnp.zeros((tm, D_MODEL), F32)
    for jj in range(D_FF // FF_CHUNK):
        a = conv_cols(jj * FF_CHUNK)
        b = conv_cols(D_FF + jj * FF_CHUNK)
        act = (a * _sigmoid(a) * b).astype(BF16)
        acc = acc + _dot(act, wdn_ref[jj * FF_CHUNK:(jj + 1) * FF_CHUNK, :])
    o_ref[...] = h1 + acc


def mixer_mlp(h, s_out, r_out, d_out, w, seq):
    n = h.shape[0]
    tm = TM_MLP

    def rows(width):
        return pl.BlockSpec((tm, width), lambda i: (i, 0))

    consts = (w["wo_s"], w["wo_r"], w["wo_d"], w["g2"], w["w_up"], w["conv_w"], w["conv_b"], w["w_down"])
    return pl.pallas_call(
        functools.partial(_mlp_kernel, tm=tm, tiles_per_seq=seq // tm),
        out_shape=jax.ShapeDtypeStruct((n, D_MODEL), F32),
        grid=(n // tm,),
        in_specs=[rows(D_MODEL), rows(SSM_WIDTH), rows(RET_WIDTH), rows(DIFF_WIDTH)]
        + [pl.BlockSpec(c.shape, lambda i: (0, 0), pipeline_mode=pl.Buffered(1)) for c in consts],
        out_specs=rows(D_MODEL),
        scratch_shapes=[pltpu.VMEM((8, 2 * D_FF), F32)],
        compiler_params=_params(("arbitrary",), 56),
        name="mixer_mlp",
    )(h, s_out, r_out, d_out, *consts)


def _ple(h_ref, p_ref, g3_ref, wpg_ref, wpe_ref):
    h2 = h_ref[...]
    gate = _sigmoid(_dot(_rms(h2, g3_ref[...]).astype(BF16), wpg_ref[...]))
    return h2 + gate * _dot(p_ref[...].astype(BF16), wpe_ref[...])


def _ple_proj_kernel(h_ref, p_ref, g3_ref, wpg_ref, wpe_ref, g1_ref, w_u, w_ret, w_dq, w_dk, w_dvt,
                     ho_ref, u_ref, ret_ref, dq_ref, dk_ref, vt_ref):
    h3 = _ple(h_ref, p_ref, g3_ref, wpg_ref, wpe_ref)
    ho_ref[...] = h3
    hn = _rms(h3, g1_ref[...]).astype(BF16)
    _project(hn, w_u, w_ret, w_dq, w_dk, w_dvt, u_ref, ret_ref, dq_ref, dk_ref, vt_ref)


def _ple_final_kernel(h_ref, p_ref, g3_ref, wpg_ref, wpe_ref, gf_ref, o_ref):
    o_ref[...] = _rms(_ple(h_ref, p_ref, g3_ref, wpg_ref, wpe_ref), gf_ref[...])


def ple_proj(h, p, w, g1_next, w_next):
    n = h.shape[0]
    tm = TM_PROJ
    shapes, specs = _proj_out(n, tm)
    consts = (w["g3"], w["w_pg"], w["w_pe"], g1_next)
    return pl.pallas_call(
        _ple_proj_kernel,
        out_shape=(jax.ShapeDtypeStruct((n, D_MODEL), F32),) + shapes,
        grid=(n // tm,),
        in_specs=[pl.BlockSpec((tm, D_MODEL), lambda i: (i, 0)), pl.BlockSpec((tm, PLE_DIM), lambda i: (i, 0))]
        + [_const_spec(c.shape) for c in consts] + _proj_weight_specs(w_next),
        out_specs=(pl.BlockSpec((tm, D_MODEL), lambda i: (i, 0)),) + specs,
        compiler_params=_params(("parallel",), 56),
        name="ple_proj",
    )(h, p, *consts, *_proj_weights(w_next))


def ple_final(h, p, w, final_g):
    n = h.shape[0]
    tm = TM_PROJ
    consts = (w["g3"], w["w_pg"], w["w_pe"], final_g)
    return pl.pallas_call(
        _ple_final_kernel,
        out_shape=jax.ShapeDtypeStruct((n, D_MODEL), F32),
        grid=(n // tm,),
        in_specs=[pl.BlockSpec((tm, D_MODEL), lambda i: (i, 0)), pl.BlockSpec((tm, PLE_DIM), lambda i: (i, 0))]
        + [_const_spec(c.shape) for c in consts],
        out_specs=pl.BlockSpec((tm, D_MODEL), lambda i: (i, 0)),
        compiler_params=_params(("parallel",), 40),
        name="ple_final",
    )(h, p, *consts)


def _layer_weights(l, norm1_g, w_in, ssm_d, ssm_w_glu, ssm_b_glu, ret_gn_g, diff_subln_g, w_out, norm2_g, w_up,
                   conv_w, conv_b, w_down, norm3_g, w_pg, w_pe):
    wi = w_in[l].astype(BF16)
    c = np.cumsum([0, SSM_WIDTH, 4 * RET_WIDTH, DIFF_WIDTH, DIFF_WIDTH, DIFF_WIDTH])
    wo = w_out[l].astype(BF16)
    return {
        "g1": norm1_g[l][None, :],
        "w_u": wi[:, c[0]:c[1]], "w_ret": wi[:, c[1]:c[2]], "w_dq": wi[:, c[2]:c[3]], "w_dk": wi[:, c[3]:c[4]],
        "w_dvt": wi[:, c[4]:c[5]].T,
        "ssm_d": ssm_d[l][None, :], "w_glu": ssm_w_glu[l].astype(BF16), "b_glu": ssm_b_glu[l][None, :],
        "gn": ret_gn_g[l][None, :], "subln_g": diff_subln_g[l][None, :],
        "wo_s": wo[:SSM_WIDTH], "wo_r": wo[SSM_WIDTH:SSM_WIDTH + RET_WIDTH], "wo_d": wo[SSM_WIDTH + RET_WIDTH:],
        "g2": norm2_g[l][None, :], "w_up": w_up[l].astype(BF16), "conv_w": conv_w[l], "conv_b": conv_b[l][None, :],
        "w_down": w_down[l].astype(BF16),
        "g3": norm3_g[l][None, :], "w_pg": w_pg[l].astype(BF16), "w_pe": w_pe[l].astype(BF16),
    }


def kernel(x, p, norm1_g, w_in, ssm_lam_re, ssm_lam_im, ssm_log_dt, ssm_b_re, ssm_b_im, ssm_c_re, ssm_c_im, ssm_d, ssm_w_glu, ssm_b_glu, ret_gn_g, diff_lq1, diff_lk1, diff_lq2, diff_lk2, diff_subln_g, w_out, norm2_g, w_up, conv_w, conv_b, w_down, norm3_g, w_pg, w_pe, final_g):
    nb, seq, _ = x.shape
    depth = w_in.shape[0]
    n = nb * seq
    assert seq % max(TM_PROJ, RET_CHUNK, ATT_TILE, S5_CHUNK, TM_MLP) == 0 and nb == 8

    ws = [_layer_weights(l, norm1_g, w_in, ssm_d, ssm_w_glu, ssm_b_glu, ret_gn_g, diff_subln_g, w_out, norm2_g,
                         w_up, conv_w, conv_b, w_down, norm3_g, w_pg, w_pe) for l in range(depth)]
    bmat, cmat, a = s5_prep(ssm_lam_re, ssm_lam_im, ssm_log_dt, ssm_b_re, ssm_b_im, ssm_c_re, ssm_c_im)
    bmat = bmat.astype(BF16)
    cmat = cmat.astype(BF16)

    h = x.reshape(n, D_MODEL)
    u, ret, dq, dk, vt = norm_proj(h, ws[0]["g1"], ws[0])
    out = None
    for l in range(depth):
        w = ws[l]
        lambda_init = 0.8 - 0.6 * math.exp(-0.3 * l)
        s_out = s5_mixer(u.reshape(nb, seq, SSM_WIDTH), bmat[l], cmat[l], a[l], w["ssm_d"], w["w_glu"], w["b_glu"])
        r_out = retention(ret, w["gn"], nb, seq)
        d_out = diff_attention(dq, dk, vt, diff_lq1[l][None, :], diff_lk1[l][None, :], diff_lq2[l][None, :],
                               diff_lk2[l][None, :], w["subln_g"], lambda_init, nb, seq)
        h = mixer_mlp(h, s_out.reshape(n, SSM_WIDTH), r_out, d_out, w, seq)
        pl_in = p[l].reshape(n, PLE_DIM)
        if l + 1 < depth:
            h, u, ret, dq, dk, vt = ple_proj(h, pl_in, w, ws[l + 1]["g1"], ws[l + 1])
        else:
            out = ple_final(h, pl_in, w, final_g[None, :])
    return out.reshape(nb, seq, D_MODEL)
```

```python
import functools
import math

import jax
import jax.numpy as jnp
import numpy as np
from jax import lax
from jax.experimental import pallas as pl
from jax.experimental.pallas import tpu as pltpu

F32 = jnp.float32
BF16 = jnp.bfloat16

D_MODEL = 1024
PLE_DIM = 256
SSM_WIDTH = 256
SSM_GROUP = 16
SSM_GROUPS = 16
SSM_STATE = 64
SSM_STATES = SSM_GROUPS * SSM_STATE
RET_HEADS = 6
RET_HEAD_DIM = 64
RET_WIDTH = RET_HEADS * RET_HEAD_DIM
DIFF_HEADS = 6
DIFF_QK_DIM = 32
DIFF_V_DIM = 64
DIFF_WIDTH = DIFF_HEADS * DIFF_V_DIM
D_FF = 2816
EPS = 1e-6
RET_LOG_GAMMA = np.log1p(-(2.0 ** (-5.0 - np.arange(RET_HEADS)))).astype(np.float32)
ALIBI_SLOPES = (2.0 ** (-8.0 * (np.arange(DIFF_HEADS) + 1) / DIFF_HEADS)).astype(np.float32)

LANES = 128
VMEM_BYTES_V7X = 64 * 1024 * 1024
NEG_BIG = -1e30
LOG2E = math.log2(math.e)

TM_PROJ = 512
TM_MLP = 256
FF_CHUNK = 256
S5_CHUNK = 128
RET_CHUNK = 256
ATT_TILE = 256
VT_TILE = 256


def _params(semantics, vmem_mib):
    assert vmem_mib * 1024 * 1024 < VMEM_BYTES_V7X
    return pltpu.CompilerParams(dimension_semantics=semantics, vmem_limit_bytes=vmem_mib * 1024 * 1024)


def _const_spec(shape):
    nd = len(shape)
    return pl.BlockSpec(shape, lambda *_: (0,) * nd)


def _rms(x, g):
    return x * lax.rsqrt(jnp.mean(x * x, axis=-1, keepdims=True) + EPS) * g


def _sigmoid(x):
    return 1.0 / (1.0 + jnp.exp(-x))


def _dot(a, b):
    return jnp.dot(a, b, preferred_element_type=F32)


def _dot_nt(a, b):
    return lax.dot_general(a, b, (((1,), (1,)), ((), ())), preferred_element_type=F32)


def _project(hn, w_u, w_ret, w_dq, w_dk, w_dvt, u_ref, ret_ref, dq_ref, dk_ref, vt_ref):
    u_ref[...] = _dot(hn, w_u[...])
    ret_ref[...] = _dot(hn, w_ret[...]).astype(BF16)
    dq_ref[...] = (_dot(hn, w_dq[...]) * (DIFF_QK_DIM ** -0.5 * LOG2E)).astype(BF16)
    dk_ref[...] = _dot(hn, w_dk[...]).astype(BF16)
    vt = _dot_nt(w_dvt[...], hn).astype(BF16)
    for r in range(vt_ref.shape[0]):
        vt_ref[r] = vt[:, r * VT_TILE:(r + 1) * VT_TILE]


def _norm_proj_kernel(h_ref, g_ref, w_u, w_ret, w_dq, w_dk, w_dvt, u_ref, ret_ref, dq_ref, dk_ref, vt_ref):
    hn = _rms(h_ref[...], g_ref[...]).astype(BF16)
    _project(hn, w_u, w_ret, w_dq, w_dk, w_dvt, u_ref, ret_ref, dq_ref, dk_ref, vt_ref)


def _proj_out(n, tm):
    shapes = (
        jax.ShapeDtypeStruct((n, SSM_WIDTH), F32),
        jax.ShapeDtypeStruct((n, 4 * RET_WIDTH), BF16),
        jax.ShapeDtypeStruct((n, DIFF_WIDTH), BF16),
        jax.ShapeDtypeStruct((n, DIFF_WIDTH), BF16),
        jax.ShapeDtypeStruct((n // VT_TILE, DIFF_WIDTH, VT_TILE), BF16),
    )
    specs = (
        pl.BlockSpec((tm, SSM_WIDTH), lambda i: (i, 0)),
        pl.BlockSpec((tm, 4 * RET_WIDTH), lambda i: (i, 0)),
        pl.BlockSpec((tm, DIFF_WIDTH), lambda i: (i, 0)),
        pl.BlockSpec((tm, DIFF_WIDTH), lambda i: (i, 0)),
        pl.BlockSpec((tm // VT_TILE, DIFF_WIDTH, VT_TILE), lambda i: (i, 0, 0)),
    )
    return shapes, specs


def _proj_weight_specs(w):
    return [_const_spec(w[k].shape) for k in ("w_u", "w_ret", "w_dq", "w_dk", "w_dvt")]


def _proj_weights(w):
    return [w[k] for k in ("w_u", "w_ret", "w_dq", "w_dk", "w_dvt")]


def norm_proj(h, g, w):
    n = h.shape[0]
    tm = TM_PROJ
    shapes, specs = _proj_out(n, tm)
    return pl.pallas_call(
        _norm_proj_kernel,
        out_shape=shapes,
        grid=(n // tm,),
        in_specs=[pl.BlockSpec((tm, D_MODEL), lambda i: (i, 0)), _const_spec(g.shape)] + _proj_weight_specs(w),
        out_specs=specs,
        compiler_params=_params(("parallel",), 48),
        name="norm_proj",
    )(h, g, *_proj_weights(w))


def _s5_prep_kernel(lr_ref, li_ref, ldt_ref, brt_ref, bit_ref, crt_ref, cit_ref, b_ref, c_ref, a_ref):
    lr = lr_ref[0]
    li = li_ref[0]
    dt = jnp.exp(ldt_ref[0])
    mag = jnp.exp(lr * dt)
    ar = mag * jnp.cos(li * dt)
    ai = mag * jnp.sin(li * dt)
    den = lr * lr + li * li
    cr = ((ar - 1.0) * lr + ai * li) / den
    ci = (ai * lr - (ar - 1.0) * li) / den
    b_ref[...] = jnp.zeros_like(b_ref)
    c_ref[...] = jnp.zeros_like(c_ref)
    for g in range(SSM_GROUPS):
        rows = slice(g * SSM_GROUP, (g + 1) * SSM_GROUP)
        cols = slice(g * SSM_STATE, (g + 1) * SSM_STATE)
        cols_im = slice(SSM_STATES + g * SSM_STATE, SSM_STATES + (g + 1) * SSM_STATE)
        crg = cr[g:g + 1, :]
        cig = ci[g:g + 1, :]
        br = brt_ref[0, g]
        bi = bit_ref[0, g]
        b_ref[0, rows, cols] = crg * br - cig * bi
        b_ref[0, rows, cols_im] = crg * bi + cig * br
        c_ref[0, cols, rows] = crt_ref[0, g]
        c_ref[0, cols_im, rows] = -cit_ref[0, g]
        a_ref[0, 0:1, cols] = ar[g:g + 1, :]
        a_ref[0, 1:2, cols] = ai[g:g + 1, :]


def s5_prep(lam_re, lam_im, log_dt, b_re, b_im, c_re, c_im):
    depth = lam_re.shape[0]
    brt = jnp.transpose(b_re, (0, 1, 3, 2))
    bit = jnp.transpose(b_im, (0, 1, 3, 2))
    crt = jnp.transpose(c_re, (0, 1, 3, 2))
    cit = jnp.transpose(c_im, (0, 1, 3, 2))
    ldt = log_dt[..., None]

    def spec(a):
        nd = a.ndim
        return pl.BlockSpec((1,) + a.shape[1:], lambda l: (l,) + (0,) * (nd - 1))

    ins = (lam_re, lam_im, ldt, brt, bit, crt, cit)
    out_shape = (
        jax.ShapeDtypeStruct((depth, SSM_WIDTH, 2 * SSM_STATES), F32),
        jax.ShapeDtypeStruct((depth, 2 * SSM_STATES, SSM_WIDTH), F32),
        jax.ShapeDtypeStruct((depth, 2, SSM_STATES), F32),
    )
    return pl.pallas_call(
        _s5_prep_kernel,
        out_shape=out_shape,
        grid=(depth,),
        in_specs=[spec(a) for a in ins],
        out_specs=tuple(spec(o) for o in out_shape),
        compiler_params=_params(("parallel",), 32),
        name="s5_prep",
    )(*ins)


def _gelu_tanh(x):
    return 0.5 * x * (1.0 + jnp.tanh(math.sqrt(2.0 / math.pi) * (x + 0.044715 * (x * x * x))))


def _s5_kernel(u_ref, bmat_ref, cmat_ref, a_ref, d_ref, wglu_ref, bglu_ref, o_ref, x_scr, st_scr, *, tc):
    nb = u_ref.shape[0]

    @pl.when(pl.program_id(0) == 0)
    def _():
        st_scr[...] = jnp.zeros_like(st_scr)

    u_tb = jnp.swapaxes(u_ref[...], 0, 1).reshape(tc * nb, SSM_WIDTH)
    x_scr[...] = _dot(u_tb.astype(BF16), bmat_ref[...])
    ar = jnp.broadcast_to(a_ref[0:1, :], (nb, SSM_STATES))
    ai = jnp.broadcast_to(a_ref[1:2, :], (nb, SSM_STATES))

    def step(t, carry):
        xr, xi = carry
        r = pl.multiple_of(t * nb, nb)
        nxr = ar * xr - ai * xi + x_scr[pl.ds(r, nb), 0:SSM_STATES]
        nxi = ar * xi + ai * xr + x_scr[pl.ds(r, nb), SSM_STATES:2 * SSM_STATES]
        x_scr[pl.ds(r, nb), 0:SSM_STATES] = nxr
        x_scr[pl.ds(r, nb), SSM_STATES:2 * SSM_STATES] = nxi
        return nxr, nxi

    xr, xi = lax.fori_loop(0, tc, step, (st_scr[:, 0:SSM_STATES], st_scr[:, SSM_STATES:2 * SSM_STATES]))
    st_scr[:, 0:SSM_STATES] = xr
    st_scr[:, SSM_STATES:2 * SSM_STATES] = xi

    y = _dot(x_scr[...].astype(BF16), cmat_ref[...]) + d_ref[...] * u_tb
    z = _gelu_tanh(y)
    out = z * _sigmoid(_dot(z.astype(BF16), wglu_ref[...]) + bglu_ref[...])
    o_ref[...] = jnp.swapaxes(out.reshape(tc, nb, SSM_WIDTH), 0, 1).astype(BF16)


def s5_mixer(u, bmat, cmat, a, d, wglu, bglu):
    nb, seq, _ = u.shape
    tc = S5_CHUNK
    blk = pl.BlockSpec((nb, tc, SSM_WIDTH), lambda c: (0, c, 0))
    consts = (bmat, cmat, a, d, wglu, bglu)
    return pl.pallas_call(
        functools.partial(_s5_kernel, tc=tc),
        out_shape=jax.ShapeDtypeStruct(u.shape, BF16),
        grid=(seq // tc,),
        in_specs=[blk] + [_const_spec(c.shape) for c in consts],
        out_specs=blk,
        scratch_shapes=[pltpu.VMEM((tc * nb, 2 * SSM_STATES), F32), pltpu.VMEM((nb, 2 * SSM_STATES), F32)],
        compiler_params=_params(("arbitrary",), 48),
        name="s5_mixer",
    )(u, *consts)


def _ret_kernel(q_ref, k_ref, v_ref, g_ref, gn_ref, o_ref, dmat_scr, decq_scr, deck_scr, sdec_scr, st_scr, *, cr):
    pairs = RET_HEADS // 2
    hd = RET_HEAD_DIM
    scale = hd ** -0.5
    lane = lax.broadcasted_iota(jnp.int32, (1, LANES), 1)
    lo = lane < hd
    rows = lax.broadcasted_iota(jnp.int32, (LANES, LANES), 0) < hd
    cols = lax.broadcasted_iota(jnp.int32, (LANES, LANES), 1) < hd
    same = rows == cols

    @pl.when((pl.program_id(0) == 0) & (pl.program_id(1) == 0))
    def _():
        t = lax.broadcasted_iota(jnp.int32, (cr, cr), 0)
        s = lax.broadcasted_iota(jnp.int32, (cr, cr), 1)
        dist = (t - s).astype(F32)
        pos = lax.broadcasted_iota(jnp.int32, (cr, LANES), 0).astype(F32)
        for h in range(RET_HEADS):
            lg = float(RET_LOG_GAMMA[h])
            dmat_scr[h] = jnp.where(dist >= 0, jnp.exp(jnp.maximum(dist, 0.0) * lg), 0.0) * scale
        for j in range(pairs):
            lga, lgb = float(RET_LOG_GAMMA[2 * j]), float(RET_LOG_GAMMA[2 * j + 1])
            lg = jnp.where(lo, lga, lgb)
            decq_scr[j] = jnp.exp((pos + 1.0) * lg)
            deck_scr[j] = jnp.exp((cr - 1.0 - pos) * lg) * scale
            sdec_scr[j] = jnp.where(same, jnp.where(rows, math.exp(cr * lga), math.exp(cr * lgb)), 0.0)

    @pl.when(pl.program_id(1) == 0)
    def _():
        st_scr[...] = jnp.zeros_like(st_scr)

    avg = jnp.where(same, 1.0 / hd, 0.0).astype(F32)
    blockmask = jnp.where(same, 1.0, 0.0).astype(F32)
    for j in range(pairs):
        sl = slice(j * LANES, (j + 1) * LANES)
        q = q_ref[:, sl]
        k = k_ref[:, sl]
        v = v_ref[:, sl]
        zero = jnp.zeros_like(q)
        sa = _dot_nt(jnp.where(lo, q, zero), k)
        sb = _dot_nt(jnp.where(lo, zero, q), k)
        pa = (sa * dmat_scr[2 * j]).astype(BF16)
        pb = (sb * dmat_scr[2 * j + 1]).astype(BF16)
        inner = _dot(pa, jnp.where(lo, v, zero)) + _dot(pb, jnp.where(lo, zero, v))
        st = st_scr[j]
        cross = _dot((q.astype(F32) * decq_scr[j]).astype(BF16), st.astype(BF16))
        kdt = (k.astype(F32) * deck_scr[j]).T.astype(BF16)
        st_scr[j] = st * sdec_scr[j] + _dot(kdt, v) * blockmask
        o = inner + cross
        mu = jnp.dot(o, avg, precision=lax.Precision.HIGHEST, preferred_element_type=F32)
        dev = o - mu
        var = jnp.dot(dev * dev, avg, precision=lax.Precision.HIGHEST, preferred_element_type=F32)
        on = dev * lax.rsqrt(var + EPS) * gn_ref[:, sl]
        gate = g_ref[:, sl].astype(F32)
        o_ref[:, sl] = (gate * _sigmoid(gate) * on).astype(BF16)


def retention(ret, gn, nb, seq):
    n = ret.shape[0]
    cr = RET_CHUNK
    nc = seq // cr
    pairs = RET_HEADS // 2

    def part(col):
        return pl.BlockSpec((cr, RET_WIDTH), lambda b, c: (b * nc + c, col))

    return pl.pallas_call(
        functools.partial(_ret_kernel, cr=cr),
        out_shape=jax.ShapeDtypeStruct((n, RET_WIDTH), BF16),
        grid=(nb, nc),
        in_specs=[part(0), part(1), part(2), part(3), _const_spec(gn.shape)],
        out_specs=pl.BlockSpec((cr, RET_WIDTH), lambda b, c: (b * nc + c, 0)),
        scratch_shapes=[
            pltpu.VMEM((RET_HEADS, cr, cr), F32),
            pltpu.VMEM((pairs, cr, LANES), F32),
            pltpu.VMEM((pairs, cr, LANES), F32),
            pltpu.VMEM((pairs, LANES, LANES), F32),
            pltpu.VMEM((pairs, LANES, LANES), F32),
        ],
        compiler_params=_params(("arbitrary", "arbitrary"), 32),
        name="retention",
    )(ret, ret, ret, ret, gn)


def _split_bf16(x, parts=3):
    out = []
    rem = np.float32(x)
    for _ in range(parts):
        hi = np.float32(rem.astype(BF16))
        out.append(float(hi))
        rem = np.float32(rem - hi)
    return out


def _diff_kernel(lq1_ref, lk1_ref, lq2_ref, lk2_ref, q_ref, k_ref, vt_ref, g_ref, o_ref,
                 acc_scr, m_scr, sa_scr, sb_scr, *, tile, lambda_init):
    j = pl.program_id(1)
    i = pl.program_id(2)
    hv = DIFF_V_DIM
    ones_rows = 16
    combos = 4
    n_split = 3

    lam = (jnp.exp(jnp.sum(lq1_ref[...] * lk1_ref[...], axis=-1, keepdims=True))
           - jnp.exp(jnp.sum(lq2_ref[...] * lk2_ref[...], axis=-1, keepdims=True)) + lambda_init)

    def pick(vals):
        return jnp.where(j == 0, vals[0], jnp.where(j == 1, vals[1], vals[2]))

    slope2 = [float(np.float32(ALIBI_SLOPES[h]) * np.float32(LOG2E)) for h in range(DIFF_HEADS)]
    parts = [_split_bf16(s, n_split) for s in slope2]
    slopes = [pick([slope2[2 * jj + hl] for jj in range(3)]) for hl in range(2)]
    pieces = [[pick([parts[2 * jj + hl][n] for jj in range(3)]) for n in range(n_split)] for hl in range(2)]

    s_loc = lax.broadcasted_iota(jnp.int32, (tile, tile), 0)
    t_loc = lax.broadcasted_iota(jnp.int32, (tile, tile), 1)
    acc_scr[...] = jnp.zeros_like(acc_scr)
    m_scr[...] = jnp.full_like(m_scr, NEG_BIG)

    q = q_ref[...]
    lane = lax.broadcasted_iota(jnp.int32, (1, LANES), 1)
    lane_t = lax.broadcasted_iota(jnp.int32, (tile, LANES), 1)
    zero = jnp.zeros_like(q)
    qstack = jnp.concatenate(
        [jnp.where((lane >= DIFF_QK_DIM * c) & (lane < DIFF_QK_DIM * (c + 1)), q, zero) for c in range(combos)], axis=0)
    qfeat = []
    for hl in range(2):
        f = jnp.zeros((tile, LANES), F32)
        for n in range(n_split):
            f = jnp.where(lane_t == n, pieces[hl][n], f)
        qfeat += [f.astype(BF16)] * 2
    q_ext = jnp.concatenate([qstack, jnp.concatenate(qfeat, axis=0)], axis=1)
    pos = lax.broadcasted_iota(jnp.int32, (tile, LANES), 0).astype(F32)
    kfeat = jnp.where(lane_t < n_split, pos, 0.0).astype(BF16)
    ones = jnp.ones((ones_rows, tile), BF16)

    def score(kvt, s_scr):
        kv0 = pl.multiple_of(kvt * tile, tile)
        k_ext = jnp.concatenate([k_ref[pl.ds(kv0, tile), :], kfeat], axis=1)
        s_scr[...] = _dot_nt(k_ext, q_ext)

    def consume(s_scr, kvt, diag):
        vt = vt_ref[kvt]
        off = ((kvt - i) * tile).astype(F32)
        for hl in range(2):
            shift = slopes[hl] * off
            ps, alphas = [], []
            for half in range(2):
                c = 2 * hl + half
                st = s_scr[:, c * tile:(c + 1) * tile]
                if diag:
                    st = jnp.where(s_loc <= t_loc, st, NEG_BIG)
                m_old = m_scr[c]
                m_new = jnp.maximum(m_old, jnp.max(st, axis=0, keepdims=True) + shift)
                ps.append(jnp.exp2(st - (m_new - shift)).astype(BF16))
                alphas.append(jnp.exp2(m_old - m_new))
                m_scr[c] = m_new
            vaug = jnp.concatenate([vt[hl * hv:(hl + 1) * hv, :], ones], axis=0)
            pv = _dot(vaug, jnp.concatenate(ps, axis=1))
            acc_scr[hl] = acc_scr[hl] * jnp.concatenate(alphas, axis=1) + pv

    score(0, sa_scr)

    def body(u, carry):
        t = 2 * u
        score(t + 1, sb_scr)
        consume(sa_scr, t, False)
        score(t + 2, sa_scr)
        consume(sb_scr, t + 1, False)
        return carry

    lax.fori_loop(0, lax.shift_right_logical(i, 1), body, 0)

    @pl.when((i & 1) == 0)
    def _():
        consume(sa_scr, i, True)

    @pl.when((i & 1) == 1)
    def _():
        score(i, sb_scr)
        consume(sa_scr, i - 1, False)
        consume(sb_scr, i, True)

    outs = []
    for hl in range(2):
        a0 = acc_scr[hl, :, 0:tile]
        a1 = acc_scr[hl, :, tile:2 * tile]
        o = a0[0:hv] / a0[hv:hv + 1] - lam * (a1[0:hv] / a1[hv:hv + 1])
        outs.append(o * lax.rsqrt(jnp.mean(o * o, axis=0, keepdims=True) + EPS))
    ot = jnp.concatenate(outs, axis=0)
    o_ref[...] = (ot.T * g_ref[...] * (1.0 - lambda_init)).astype(BF16)


def diff_attention(dq, dk, vt, lq1, lk1, lq2, lk2, subln_g, lambda_init, nb, seq):
    n = dq.shape[0]
    tile = ATT_TILE
    nq = seq // tile
    pairs = DIFF_HEADS // 2
    lam_spec = _const_spec(lq1.shape)
    return pl.pallas_call(
        functools.partial(_diff_kernel, tile=tile, lambda_init=lambda_init),
        out_shape=jax.ShapeDtypeStruct((n, DIFF_WIDTH), BF16),
        grid=(nb, pairs, nq),
        in_specs=[
            lam_spec, lam_spec, lam_spec, lam_spec,
            pl.BlockSpec((tile, LANES), lambda b, j, i: (b * nq + i, j)),
            pl.BlockSpec((seq, LANES), lambda b, j, i: (b, j)),
            pl.BlockSpec((seq // VT_TILE, LANES, VT_TILE), lambda b, j, i: (b, j, 0)),
            pl.BlockSpec((1, LANES), lambda b, j, i: (0, j)),
        ],
        out_specs=pl.BlockSpec((tile, LANES), lambda b, j, i: (b * nq + i, j)),
        scratch_shapes=[
            pltpu.VMEM((2, DIFF_V_DIM + 16, 2 * tile), F32),
            pltpu.VMEM((4, 1, tile), F32),
            pltpu.VMEM((tile, 4 * tile), F32),
            pltpu.VMEM((tile, 4 * tile), F32),
        ],
        compiler_params=_params(("parallel", "parallel", "arbitrary"), 32),
        name="diff_attention",
    )(lq1, lk1, lq2, lk2, dq, dk, vt, subln_g)


def _mlp_kernel(h_ref, s_ref, r_ref, d_ref, wos_ref, wor_ref, wod_ref, g2_ref, wup_ref, cw_ref, cb_ref, wdn_ref,
                o_ref, carry_scr, *, tm, tiles_per_seq):
    @pl.when(pl.program_id(0) % tiles_per_seq == 0)
    def _():
        carry_scr[...] = jnp.zeros_like(carry_scr)

    h1 = (h_ref[...] + _dot(s_ref[...], wos_ref[...]) + _dot(r_ref[...], wor_ref[...])
          + _dot(d_ref[...], wod_ref[...]))
    hn = _rms(h1, g2_ref[...]).astype(BF16)
    row8 = lax.broadcasted_iota(jnp.int32, (8, FF_CHUNK), 0)

    def conv_cols(c0):
        u = _dot(hn, wup_ref[:, c0:c0 + FF_CHUNK])
        c6 = carry_scr[6:7, c0:c0 + FF_CHUNK]
        c7 = carry_scr[7:8, c0:c0 + FF_CHUNK]
        r1 = pltpu.roll(u, 1, 0)
        r2 = pltpu.roll(u, 2, 0)
        p1 = jnp.concatenate([jnp.where(row8 == 0, c7, r1[0:8]), r1[8:]], axis=0)
        p2 = jnp.concatenate([jnp.where(row8 == 0, c6, jnp.where(row8 == 1, c7, r2[0:8])), r2[8:]], axis=0)
        carry_scr[:, c0:c0 + FF_CHUNK] = u[tm - 8:tm, :]
        w = cw_ref[:, c0:c0 + FF_CHUNK]
        return w[0:1] * p2 + w[1:2] * p1 + w[2:3] * u + cb_ref[:, c0:c0 + FF_CHUNK]

    acc = jnp.zeros((tm, D_MODEL), F32)
    for jj in range(D_FF // FF_CHUNK):
        a = conv_cols(jj * FF_CHUNK)
        b = conv_cols(D_FF + jj * FF_CHUNK)
        act = (a * _sigmoid(a) * b).astype(BF16)
        acc = acc + _dot(act, wdn_ref[jj * FF_CHUNK:(jj + 1) * FF_CHUNK, :])
    o_ref[...] = h1 + acc


def mixer_mlp(h, s_out, r_out, d_out, w, seq):
    n = h.shape[0]
    tm = TM_MLP

    def rows(width):
        return pl.BlockSpec((tm, width), lambda i: (i, 0))

    consts = (w["wo_s"], w["wo_r"], w["wo_d"], w["g2"], w["w_up"], w["conv_w"], w["conv_b"], w["w_down"])
    return pl.pallas_call(
        functools.partial(_mlp_kernel, tm=tm, tiles_per_seq=seq // tm),
        out_shape=jax.ShapeDtypeStruct((n, D_MODEL), F32),
        grid=(n // tm,),
        in_specs=[rows(D_MODEL), rows(SSM_WIDTH), rows(RET_WIDTH), rows(DIFF_WIDTH)]
        + [pl.BlockSpec(c.shape, lambda i: (0, 0), pipeline_mode=pl.Buffered(1)) for c in consts],
        out_specs=rows(D_MODEL),
        scratch_shapes=[pltpu.VMEM((8, 2 * D_FF), F32)],
        compiler_params=_params(("arbitrary",), 56),
        name="mixer_mlp",
    )(h, s_out, r_out, d_out, *consts)


def _ple(h_ref, p_ref, g3_ref, wpg_ref, wpe_ref):
    h2 = h_ref[...]
    gate = _sigmoid(_dot(_rms(h2, g3_ref[...]).astype(BF16), wpg_ref[...]))
    return h2 + gate * _dot(p_ref[...].astype(BF16), wpe_ref[...])


def _ple_proj_kernel(h_ref, p_ref, g3_ref, wpg_ref, wpe_ref, g1_ref, w_u, w_ret, w_dq, w_dk, w_dvt,
                     ho_ref, u_ref, ret_ref, dq_ref, dk_ref, vt_ref):
    h3 = _ple(h_ref, p_ref, g3_ref, wpg_ref, wpe_ref)
    ho_ref[...] = h3
    hn = _rms(h3, g1_ref[...]).astype(BF16)
    _project(hn, w_u, w_ret, w_dq, w_dk, w_dvt, u_ref, ret_ref, dq_ref, dk_ref, vt_ref)


def _ple_final_kernel(h_ref, p_ref, g3_ref, wpg_ref, wpe_ref, gf_ref, o_ref):
    o_ref[...] = _rms(_ple(h_ref, p_ref, g3_ref, wpg_ref, wpe_ref), gf_ref[...])


def ple_proj(h, p, w, g1_next, w_next):
    n = h.shape[0]
    tm = TM_PROJ
    shapes, specs = _proj_out(n, tm)
    consts = (w["g3"], w["w_pg"], w["w_pe"], g1_next)
    return pl.pallas_call(
        _ple_proj_kernel,
        out_shape=(jax.ShapeDtypeStruct((n, D_MODEL), F32),) + shapes,
        grid=(n // tm,),
        in_specs=[pl.BlockSpec((tm, D_MODEL), lambda i: (i, 0)), pl.BlockSpec((tm, PLE_DIM), lambda i: (i, 0))]
        + [_const_spec(c.shape) for c in consts] + _proj_weight_specs(w_next),
        out_specs=(pl.BlockSpec((tm, D_MODEL), lambda i: (i, 0)),) + specs,
        compiler_params=_params(("parallel",), 56),
        name="ple_proj",
    )(h, p, *consts, *_proj_weights(w_next))


def ple_final(h, p, w, final_g):
    n = h.shape[0]
    tm = TM_PROJ
    consts = (w["g3"], w["w_pg"], w["w_pe"], final_g)
    return pl.pallas_call(
        _ple_final_kernel,
        out_shape=jax.ShapeDtypeStruct((n, D_MODEL), F32),
        grid=(n // tm,),
        in_specs=[pl.BlockSpec((tm, D_MODEL), lambda i: (i, 0)), pl.BlockSpec((tm, PLE_DIM), lambda i: (i, 0))]
        + [_const_spec(c.shape) for c in consts],
        out_specs=pl.BlockSpec((tm, D_MODEL), lambda i: (i, 0)),
        compiler_params=_params(("parallel",), 40),
        name="ple_final",
    )(h, p, *consts)


def _layer_weights(l, norm1_g, w_in, ssm_d, ssm_w_glu, ssm_b_glu, ret_gn_g, diff_subln_g, w_out, norm2_g, w_up,
                   conv_w, conv_b, w_down, norm3_g, w_pg, w_pe):
    wi = w_in[l].astype(BF16)
    c = np.cumsum([0, SSM_WIDTH, 4 * RET_WIDTH, DIFF_WIDTH, DIFF_WIDTH, DIFF_WIDTH])
    wo = w_out[l].astype(BF16)
    return {
        "g1": norm1_g[l][None, :],
        "w_u": wi[:, c[0]:c[1]], "w_ret": wi[:, c[1]:c[2]], "w_dq": wi[:, c[2]:c[3]], "w_dk": wi[:, c[3]:c[4]],
        "w_dvt": wi[:, c[4]:c[5]].T,
        "ssm_d": ssm_d[l][None, :], "w_glu": ssm_w_glu[l].astype(BF16), "b_glu": ssm_b_glu[l][None, :],
        "gn": ret_gn_g[l][None, :], "subln_g": diff_subln_g[l][None, :],
        "wo_s": wo[:SSM_WIDTH], "wo_r": wo[SSM_WIDTH:SSM_WIDTH + RET_WIDTH], "wo_d": wo[SSM_WIDTH + RET_WIDTH:],
        "g2": norm2_g[l][None, :], "w_up": w_up[l].astype(BF16), "conv_w": conv_w[l], "conv_b": conv_b[l][None, :],
        "w_down": w_down[l].astype(BF16),
        "g3": norm3_g[l][None, :], "w_pg": w_pg[l].astype(BF16), "w_pe": w_pe[l].astype(BF16),
    }


def kernel(x, p, norm1_g, w_in, ssm_lam_re, ssm_lam_im, ssm_log_dt, ssm_b_re, ssm_b_im, ssm_c_re, ssm_c_im, ssm_d, ssm_w_glu, ssm_b_glu, ret_gn_g, diff_lq1, diff_lk1, diff_lq2, diff_lk2, diff_subln_g, w_out, norm2_g, w_up, conv_w, conv_b, w_down, norm3_g, w_pg, w_pe, final_g):
    nb, seq, _ = x.shape
    depth = w_in.shape[0]
    n = nb * seq
    assert seq % max(TM_PROJ, RET_CHUNK, ATT_TILE, S5_CHUNK, TM_MLP) == 0 and nb == 8

    ws = [_layer_weights(l, norm1_g, w_in, ssm_d, ssm_w_glu, ssm_b_glu, ret_gn_g, diff_subln_g, w_out, norm2_g,
                         w_up, conv_w, conv_b, w_down, norm3_g, w_pg, w_pe) for l in range(depth)]
    bmat, cmat, a = s5_prep(ssm_lam_re, ssm_lam_im, ssm_log_dt, ssm_b_re, ssm_b_im, ssm_c_re, ssm_c_im)
    bmat = bmat.astype(BF16)
    cmat = cmat.astype(BF16)

    h = x.reshape(n, D_MODEL)
    u, ret, dq, dk, vt = norm_proj(h, ws[0]["g1"], ws[0])
    out = None
    for l in range(depth):
        w = ws[l]
        lambda_init = 0.8 - 0.6 * math.exp(-0.3 * l)
        s_out = s5_mixer(u.reshape(nb, seq, SSM_WIDTH), bmat[l], cmat[l], a[l], w["ssm_d"], w["w_glu"], w["b_glu"])
        r_out = retention(ret, w["gn"], nb, seq)
        d_out = diff_attention(dq, dk, vt, diff_lq1[l][None, :], diff_lk1[l][None, :], diff_lq2[l][None, :],
                               diff_lk2[l][None, :], w["subln_g"], lambda_init, nb, seq)
        h = mixer_mlp(h, s_out.reshape(n, SSM_WIDTH), r_out, d_out, w, seq)
        pl_in = p[l].reshape(n, PLE_DIM)
        if l + 1 < depth:
            h, u, ret, dq, dk, vt = ple_proj(h, pl_in, w, ws[l + 1]["g1"], ws[l + 1])
        else:
            out = ple_final(h, pl_in, w, final_g[None, :])
    return out.reshape(nb, seq, D_MODEL)
```

```python
import functools
import math

import jax
import jax.numpy as jnp
import numpy as np
from jax import lax
from jax.experimental import pallas as pl
from jax.experimental.pallas import tpu as pltpu

F32 = jnp.float32
BF16 = jnp.bfloat16

D_MODEL = 1024
PLE_DIM = 256
SSM_WIDTH = 256
SSM_GROUP = 16
SSM_GROUPS = 16
SSM_STATE = 64
SSM_STATES = SSM_GROUPS * SSM_STATE
RET_HEADS = 6
RET_HEAD_DIM = 64
RET_WIDTH = RET_HEADS * RET_HEAD_DIM
DIFF_HEADS = 6
DIFF_QK_DIM = 32
DIFF_V_DIM = 64
DIFF_WIDTH = DIFF_HEADS * DIFF_V_DIM
D_FF = 2816
EPS = 1e-6
RET_LOG_GAMMA = np.log1p(-(2.0 ** (-5.0 - np.arange(RET_HEADS)))).astype(np.float32)
ALIBI_SLOPES = (2.0 ** (-8.0 * (np.arange(DIFF_HEADS) + 1) / DIFF_HEADS)).astype(np.float32)

LANES = 128
VMEM_BYTES_V7X = 64 * 1024 * 1024
NEG_BIG = -1e30
LOG2E = math.log2(math.e)

TM_PROJ = 512
TM_MLP = 256
FF_CHUNK = 256
S5_CHUNK = 128
RET_CHUNK = 256
ATT_TILE = 256
VT_TILE = 256


def _params(semantics, vmem_mib):
    assert vmem_mib * 1024 * 1024 < VMEM_BYTES_V7X
    return pltpu.CompilerParams(dimension_semantics=semantics, vmem_limit_bytes=vmem_mib * 1024 * 1024)


def _const_spec(shape):
    nd = len(shape)
    return pl.BlockSpec(shape, lambda *_: (0,) * nd)


def _rms(x, g):
    return x * lax.rsqrt(jnp.mean(x * x, axis=-1, keepdims=True) + EPS) * g


def _sigmoid(x):
    return 1.0 / (1.0 + jnp.exp(-x))


def _dot(a, b):
    return jnp.dot(a, b, preferred_element_type=F32)


def _dot_nt(a, b):
    return lax.dot_general(a, b, (((1,), (1,)), ((), ())), preferred_element_type=F32)


def _project(hn, w_u, w_ret, w_dq, w_dk, w_dvt, u_ref, ret_ref, dq_ref, dk_ref, vt_ref):
    u_ref[...] = _dot(hn, w_u[...])
    ret_ref[...] = _dot(hn, w_ret[...]).astype(BF16)
    dq_ref[...] = (_dot(hn, w_dq[...]) * (DIFF_QK_DIM ** -0.5 * LOG2E)).astype(BF16)
    dk_ref[...] = _dot(hn, w_dk[...]).astype(BF16)
    vt = _dot_nt(w_dvt[...], hn).astype(BF16)
    for r in range(vt_ref.shape[0]):
        vt_ref[r] = vt[:, r * VT_TILE:(r + 1) * VT_TILE]


def _norm_proj_kernel(h_ref, g_ref, w_u, w_ret, w_dq, w_dk, w_dvt, u_ref, ret_ref, dq_ref, dk_ref, vt_ref):
    hn = _rms(h_ref[...], g_ref[...]).astype(BF16)
    _project(hn, w_u, w_ret, w_dq, w_dk, w_dvt, u_ref, ret_ref, dq_ref, dk_ref, vt_ref)


def _proj_out(n, tm):
    shapes = (
        jax.ShapeDtypeStruct((n, SSM_WIDTH), F32),
        jax.ShapeDtypeStruct((n, 4 * RET_WIDTH), BF16),
        jax.ShapeDtypeStruct((n, DIFF_WIDTH), BF16),
        jax.ShapeDtypeStruct((n, DIFF_WIDTH), BF16),
        jax.ShapeDtypeStruct((n // VT_TILE, DIFF_WIDTH, VT_TILE), BF16),
    )
    specs = (
        pl.BlockSpec((tm, SSM_WIDTH), lambda i: (i, 0)),
        pl.BlockSpec((tm, 4 * RET_WIDTH), lambda i: (i, 0)),
        pl.BlockSpec((tm, DIFF_WIDTH), lambda i: (i, 0)),
        pl.BlockSpec((tm, DIFF_WIDTH), lambda i: (i, 0)),
        pl.BlockSpec((tm // VT_TILE, DIFF_WIDTH, VT_TILE), lambda i: (i, 0, 0)),
    )
    return shapes, specs


def _proj_weight_specs(w):
    return [_const_spec(w[k].shape) for k in ("w_u", "w_ret", "w_dq", "w_dk", "w_dvt")]


def _proj_weights(w):
    return [w[k] for k in ("w_u", "w_ret", "w_dq", "w_dk", "w_dvt")]


def norm_proj(h, g, w):
    n = h.shape[0]
    tm = TM_PROJ
    shapes, specs = _proj_out(n, tm)
    return pl.pallas_call(
        _norm_proj_kernel,
        out_shape=shapes,
        grid=(n // tm,),
        in_specs=[pl.BlockSpec((tm, D_MODEL), lambda i: (i, 0)), _const_spec(g.shape)] + _proj_weight_specs(w),
        out_specs=specs,
        compiler_params=_params(("parallel",), 48),
        name="norm_proj",
    )(h, g, *_proj_weights(w))


def _s5_prep_kernel(lr_ref, li_ref, ldt_ref, brt_ref, bit_ref, crt_ref, cit_ref, b_ref, c_ref, a_ref):
    lr = lr_ref[0]
    li = li_ref[0]
    dt = jnp.exp(ldt_ref[0])
    mag = jnp.exp(lr * dt)
    ar = mag * jnp.cos(li * dt)
    ai = mag * jnp.sin(li * dt)
    den = lr * lr + li * li
    cr = ((ar - 1.0) * lr + ai * li) / den
    ci = (ai * lr - (ar - 1.0) * li) / den
    b_ref[...] = jnp.zeros_like(b_ref)
    c_ref[...] = jnp.zeros_like(c_ref)
    for g in range(SSM_GROUPS):
        rows = slice(g * SSM_GROUP, (g + 1) * SSM_GROUP)
        cols = slice(g * SSM_STATE, (g + 1) * SSM_STATE)
        cols_im = slice(SSM_STATES + g * SSM_STATE, SSM_STATES + (g + 1) * SSM_STATE)
        crg = cr[g:g + 1, :]
        cig = ci[g:g + 1, :]
        br = brt_ref[0, g]
        bi = bit_ref[0, g]
        b_ref[0, rows, cols] = crg * br - cig * bi
        b_ref[0, rows, cols_im] = crg * bi + cig * br
        c_ref[0, cols, rows] = crt_ref[0, g]
        c_ref[0, cols_im, rows] = -cit_ref[0, g]
        a_ref[0, 0:1, cols] = ar[g:g + 1, :]
        a_ref[0, 1:2, cols] = ai[g:g + 1, :]


def s5_prep(lam_re, lam_im, log_dt, b_re, b_im, c_re, c_im):
    depth = lam_re.shape[0]
    brt = jnp.transpose(b_re, (0, 1, 3, 2))
    bit = jnp.transpose(b_im, (0, 1, 3, 2))
    crt = jnp.transpose(c_re, (0, 1, 3, 2))
    cit = jnp.transpose(c_im, (0, 1, 3, 2))
    ldt = log_dt[..., None]

    def spec(a):
        nd = a.ndim
        return pl.BlockSpec((1,) + a.shape[1:], lambda l: (l,) + (0,) * (nd - 1))

    ins = (lam_re, lam_im, ldt, brt, bit, crt, cit)
    out_shape = (
        jax.ShapeDtypeStruct((depth, SSM_WIDTH, 2 * SSM_STATES), F32),
        jax.ShapeDtypeStruct((depth, 2 * SSM_STATES, SSM_WIDTH), F32),
        jax.ShapeDtypeStruct((depth, 2, SSM_STATES), F32),
    )
    return pl.pallas_call(
        _s5_prep_kernel,
        out_shape=out_shape,
        grid=(depth,),
        in_specs=[spec(a) for a in ins],
        out_specs=tuple(spec(o) for o in out_shape),
        compiler_params=_params(("parallel",), 32),
        name="s5_prep",
    )(*ins)


def _gelu_tanh(x):
    return 0.5 * x * (1.0 + jnp.tanh(math.sqrt(2.0 / math.pi) * (x + 0.044715 * (x * x * x))))


def _s5_kernel(u_ref, bmat_ref, cmat_ref, a_ref, d_ref, wglu_ref, bglu_ref, o_ref, x_scr, st_scr, *, tc):
    nb = u_ref.shape[0]

    @pl.when(pl.program_id(0) == 0)
    def _():
        st_scr[...] = jnp.zeros_like(st_scr)

    u_tb = jnp.swapaxes(u_ref[...], 0, 1).reshape(tc * nb, SSM_WIDTH)
    x_scr[...] = _dot(u_tb.astype(BF16), bmat_ref[...])
    ar = jnp.broadcast_to(a_ref[0:1, :], (nb, SSM_STATES))
    ai = jnp.broadcast_to(a_ref[1:2, :], (nb, SSM_STATES))

    def step(t, carry):
        xr, xi = carry
        r = pl.multiple_of(t * nb, nb)
        nxr = ar * xr - ai * xi + x_scr[pl.ds(r, nb), 0:SSM_STATES]
        nxi = ar * xi + ai * xr + x_scr[pl.ds(r, nb), SSM_STATES:2 * SSM_STATES]
        x_scr[pl.ds(r, nb), 0:SSM_STATES] = nxr
        x_scr[pl.ds(r, nb), SSM_STATES:2 * SSM_STATES] = nxi
        return nxr, nxi

    xr, xi = lax.fori_loop(0, tc, step, (st_scr[:, 0:SSM_STATES], st_scr[:, SSM_STATES:2 * SSM_STATES]))
    st_scr[:, 0:SSM_STATES] = xr
    st_scr[:, SSM_STATES:2 * SSM_STATES] = xi

    y = _dot(x_scr[...].astype(BF16), cmat_ref[...]) + d_ref[...] * u_tb
    z = _gelu_tanh(y)
    out = z * _sigmoid(_dot(z.astype(BF16), wglu_ref[...]) + bglu_ref[...])
    o_ref[...] = jnp.swapaxes(out.reshape(tc, nb, SSM_WIDTH), 0, 1).astype(BF16)


def s5_mixer(u, bmat, cmat, a, d, wglu, bglu):
    nb, seq, _ = u.shape
    tc = S5_CHUNK
    blk = pl.BlockSpec((nb, tc, SSM_WIDTH), lambda c: (0, c, 0))
    consts = (bmat, cmat, a, d, wglu, bglu)
    return pl.pallas_call(
        functools.partial(_s5_kernel, tc=tc),
        out_shape=jax.ShapeDtypeStruct(u.shape, BF16),
        grid=(seq // tc,),
        in_specs=[blk] + [_const_spec(c.shape) for c in consts],
        out_specs=blk,
        scratch_shapes=[pltpu.VMEM((tc * nb, 2 * SSM_STATES), F32), pltpu.VMEM((nb, 2 * SSM_STATES), F32)],
        compiler_params=_params(("arbitrary",), 48),
        name="s5_mixer",
    )(u, *consts)


def _ret_kernel(q_ref, k_ref, v_ref, g_ref, gn_ref, o_ref, dmat_scr, decq_scr, deck_scr, sdec_scr, st_scr, *, cr):
    pairs = RET_HEADS // 2
    hd = RET_HEAD_DIM
    scale = hd ** -0.5
    lane = lax.broadcasted_iota(jnp.int32, (1, LANES), 1)
    lo = lane < hd
    rows = lax.broadcasted_iota(jnp.int32, (LANES, LANES), 0) < hd
    cols = lax.broadcasted_iota(jnp.int32, (LANES, LANES), 1) < hd
    same = rows == cols

    @pl.when((pl.program_id(0) == 0) & (pl.program_id(1) == 0))
    def _():
        t = lax.broadcasted_iota(jnp.int32, (cr, cr), 0)
        s = lax.broadcasted_iota(jnp.int32, (cr, cr), 1)
        dist = (t - s).astype(F32)
        pos = lax.broadcasted_iota(jnp.int32, (cr, LANES), 0).astype(F32)
        for h in range(RET_HEADS):
            lg = float(RET_LOG_GAMMA[h])
            dmat_scr[h // 2, :, (h % 2) * cr:(h % 2 + 1) * cr] = (
                jnp.where(dist >= 0, jnp.exp(jnp.maximum(dist, 0.0) * lg), 0.0) * scale)
        for j in range(pairs):
            lga, lgb = float(RET_LOG_GAMMA[2 * j]), float(RET_LOG_GAMMA[2 * j + 1])
            lg = jnp.where(lo, lga, lgb)
            decq_scr[j] = jnp.exp((pos + 1.0) * lg)
            deck_scr[j] = jnp.exp((cr - 1.0 - pos) * lg) * scale
            sdec_scr[j] = jnp.where(same, jnp.where(rows, math.exp(cr * lga), math.exp(cr * lgb)), 0.0)

    @pl.when(pl.program_id(1) == 0)
    def _():
        st_scr[...] = jnp.zeros_like(st_scr)

    avg2 = jnp.where(jnp.concatenate([same, same], axis=0), 1.0 / hd, 0.0).astype(BF16)
    blockmask = jnp.where(same, 1.0, 0.0).astype(F32)

    def group_mean(x):
        hi = x.astype(BF16)
        lo_part = (x - hi.astype(F32)).astype(BF16)
        return _dot(jnp.concatenate([hi, lo_part], axis=1), avg2)

    sls = [slice(j * LANES, (j + 1) * LANES) for j in range(pairs)]
    scores = []
    for j in range(pairs):
        k = k_ref[:, sls[j]]
        zero = jnp.zeros_like(k)
        k2 = jnp.concatenate([jnp.where(lo, k, zero), jnp.where(lo, zero, k)], axis=0)
        scores.append(_dot_nt(q_ref[:, sls[j]], k2))
    probs = [(scores[j] * dmat_scr[j]).astype(BF16) for j in range(pairs)]
    outs = []
    for j in range(pairs):
        q = q_ref[:, sls[j]]
        k = k_ref[:, sls[j]]
        v = v_ref[:, sls[j]]
        zero = jnp.zeros_like(v)
        v2 = jnp.concatenate([jnp.where(lo, v, zero), jnp.where(lo, zero, v)], axis=0)
        st = st_scr[j]
        cross = _dot((q.astype(F32) * decq_scr[j]).astype(BF16), st.astype(BF16))
        kdt = (k.astype(F32) * deck_scr[j]).T.astype(BF16)
        st_scr[j] = st * sdec_scr[j] + _dot(kdt, v) * blockmask
        outs.append(_dot(probs[j], v2) + cross)
    devs = [o - group_mean(o) for o in outs]
    variances = [group_mean(d * d) for d in devs]
    for j in range(pairs):
        on = devs[j] * lax.rsqrt(variances[j] + EPS) * gn_ref[:, sls[j]]
        gate = g_ref[:, sls[j]].astype(F32)
        o_ref[:, sls[j]] = (gate * _sigmoid(gate) * on).astype(BF16)


def retention(ret, gn, nb, seq):
    n = ret.shape[0]
    cr = RET_CHUNK
    nc = seq // cr
    pairs = RET_HEADS // 2

    def part(col):
        return pl.BlockSpec((cr, RET_WIDTH), lambda b, c: (b * nc + c, col))

    return pl.pallas_call(
        functools.partial(_ret_kernel, cr=cr),
        out_shape=jax.ShapeDtypeStruct((n, RET_WIDTH), BF16),
        grid=(nb, nc),
        in_specs=[part(0), part(1), part(2), part(3), _const_spec(gn.shape)],
        out_specs=pl.BlockSpec((cr, RET_WIDTH), lambda b, c: (b * nc + c, 0)),
        scratch_shapes=[
            pltpu.VMEM((pairs, cr, 2 * cr), F32),
            pltpu.VMEM((pairs, cr, LANES), F32),
            pltpu.VMEM((pairs, cr, LANES), F32),
            pltpu.VMEM((pairs, LANES, LANES), F32),
            pltpu.VMEM((pairs, LANES, LANES), F32),
        ],
        compiler_params=_params(("arbitrary", "arbitrary"), 32),
        name="retention",
    )(ret, ret, ret, ret, gn)


def _split_bf16(x, parts=3):
    out = []
    rem = np.float32(x)
    for _ in range(parts):
        hi = np.float32(rem.astype(BF16))
        out.append(float(hi))
        rem = np.float32(rem - hi)
    return out


def _diff_kernel(lq1_ref, lk1_ref, lq2_ref, lk2_ref, q_ref, k_ref, vt_ref, g_ref, o_ref,
                 acc_scr, m_scr, s_scr, mt_scr, *, tile, lambda_init):
    j = pl.program_id(1)
    i = pl.program_id(2)
    hv = DIFF_V_DIM
    ones_rows = 16
    combos = 4
    n_split = 3

    lam = (jnp.exp(jnp.sum(lq1_ref[...] * lk1_ref[...], axis=-1, keepdims=True))
           - jnp.exp(jnp.sum(lq2_ref[...] * lk2_ref[...], axis=-1, keepdims=True)) + lambda_init)

    def pick(vals):
        return jnp.where(j == 0, vals[0], jnp.where(j == 1, vals[1], vals[2]))

    slope2 = [float(np.float32(ALIBI_SLOPES[h]) * np.float32(LOG2E)) for h in range(DIFF_HEADS)]
    parts = [_split_bf16(s, n_split) for s in slope2]
    slopes = [pick([slope2[2 * jj + hl] for jj in range(3)]) for hl in range(2)]
    pieces = [[pick([parts[2 * jj + hl][n] for jj in range(3)]) for n in range(n_split)] for hl in range(2)]

    s_loc = lax.broadcasted_iota(jnp.int32, (tile, combos * tile), 0)
    t_loc = lax.broadcasted_iota(jnp.int32, (tile, combos * tile), 1) & (tile - 1)
    causal4 = s_loc <= t_loc
    acc_scr[...] = jnp.zeros_like(acc_scr)
    m_scr[...] = jnp.full_like(m_scr, NEG_BIG)

    qt = q_ref[...].astype(F32).T
    row_t = lax.broadcasted_iota(jnp.int32, (LANES, tile), 0)
    lane_t = lax.broadcasted_iota(jnp.int32, (tile, LANES), 1)
    qstack = jnp.concatenate(
        [jnp.where((row_t >= DIFF_QK_DIM * c) & (row_t < DIFF_QK_DIM * (c + 1)), qt, 0.0) for c in range(combos)], axis=1)
    qfeat = []
    for hl in range(2):
        f = jnp.zeros((LANES, tile), F32)
        for n in range(n_split):
            f = jnp.where(row_t == n, pieces[hl][n], f)
        qfeat += [f] * 2
    qt_ext = jnp.concatenate([qstack, jnp.concatenate(qfeat, axis=1)], axis=0).astype(BF16)
    pos = lax.broadcasted_iota(jnp.int32, (tile, LANES), 0).astype(F32)
    kfeat = jnp.where(lane_t < n_split, pos, 0.0).astype(BF16)
    ones = jnp.ones((ones_rows, tile), BF16)

    def score(kvt, buf, diag=False):
        kv0 = pl.multiple_of(kvt * tile, tile)
        k_ext = jnp.concatenate([k_ref[pl.ds(kv0, tile), :], kfeat], axis=1)
        s_all = _dot(k_ext, qt_ext)
        if diag:
            s_all = jnp.where(causal4, s_all, NEG_BIG)
        s_scr[buf] = s_all
        for c in range(combos):
            mt_scr[buf, c] = jnp.max(s_all[:, c * tile:(c + 1) * tile], axis=0, keepdims=True)

    def softmax(buf, kvt):
        off = ((kvt - i) * tile).astype(F32)
        ps, alphas = [], []
        for c in range(combos):
            shift = slopes[c // 2] * off
            m_old = m_scr[c]
            m_new = jnp.maximum(m_old, mt_scr[buf, c] + shift)
            ps.append(jnp.exp2(s_scr[buf, :, c * tile:(c + 1) * tile] - (m_new - shift)).astype(BF16))
            alphas.append(jnp.exp2(m_old - m_new))
            m_scr[c] = m_new
        return ps, alphas

    def accumulate(kvt, ps, alphas):
        vt = vt_ref[kvt]
        for hl in range(2):
            vaug = jnp.concatenate([vt[hl * hv:(hl + 1) * hv, :], ones], axis=0)
            pv = _dot(vaug, jnp.concatenate(ps[2 * hl:2 * hl + 2], axis=1))
            acc_scr[hl] = acc_scr[hl] * jnp.concatenate(alphas[2 * hl:2 * hl + 2], axis=1) + pv

    def consume(buf, kvt):
        accumulate(kvt, *softmax(buf, kvt))

    score(i, 1, diag=True)
    score(0, 0)
    consume(1, i)

    def body(u, carry):
        t = 2 * u
        pa = softmax(0, t)
        score(t + 1, 1)
        accumulate(t, *pa)
        pb = softmax(1, t + 1)
        score(t + 2, 0)
        accumulate(t + 1, *pb)
        return carry

    lax.fori_loop(0, lax.shift_right_logical(jnp.maximum(i - 1, 0), 1), body, 0)

    @pl.when((i & 1) == 1)
    def _():
        consume(0, i - 1)

    @pl.when(((i & 1) == 0) & (i > 0))
    def _():
        score(i - 1, 1)
        consume(0, i - 2)
        consume(1, i - 1)

    outs = []
    for hl in range(2):
        a0 = acc_scr[hl, :, 0:tile]
        a1 = acc_scr[hl, :, tile:2 * tile]
        o = a0[0:hv] / a0[hv:hv + 1] - lam * (a1[0:hv] / a1[hv:hv + 1])
        outs.append(o * lax.rsqrt(jnp.mean(o * o, axis=0, keepdims=True) + EPS))
    ot = jnp.concatenate(outs, axis=0)
    o_ref[...] = (ot.T * g_ref[...] * (1.0 - lambda_init)).astype(BF16)


def diff_attention(dq, dk, vt, lq1, lk1, lq2, lk2, subln_g, lambda_init, nb, seq):
    n = dq.shape[0]
    tile = ATT_TILE
    nq = seq // tile
    pairs = DIFF_HEADS // 2
    lam_spec = _const_spec(lq1.shape)
    return pl.pallas_call(
        functools.partial(_diff_kernel, tile=tile, lambda_init=lambda_init),
        out_shape=jax.ShapeDtypeStruct((n, DIFF_WIDTH), BF16),
        grid=(nb, pairs, nq),
        in_specs=[
            lam_spec, lam_spec, lam_spec, lam_spec,
            pl.BlockSpec((tile, LANES), lambda b, j, i: (b * nq + i, j)),
            pl.BlockSpec((seq, LANES), lambda b, j, i: (b, j)),
            pl.BlockSpec((seq // VT_TILE, LANES, VT_TILE), lambda b, j, i: (b, j, 0)),
            pl.BlockSpec((1, LANES), lambda b, j, i: (0, j)),
        ],
        out_specs=pl.BlockSpec((tile, LANES), lambda b, j, i: (b * nq + i, j)),
        scratch_shapes=[
            pltpu.VMEM((2, DIFF_V_DIM + 16, 2 * tile), F32),
            pltpu.VMEM((4, 1, tile), F32),
            pltpu.VMEM((2, tile, 4 * tile), F32),
            pltpu.VMEM((2, 4, 1, tile), F32),
        ],
        compiler_params=_params(("parallel", "parallel", "arbitrary"), 32),
        name="diff_attention",
    )(lq1, lk1, lq2, lk2, dq, dk, vt, subln_g)


def _mlp_kernel(h_ref, s_ref, r_ref, d_ref, wo_ref, g2_ref, wup_ref, cw_ref, cb_ref, wdn_ref,
                o_ref, carry_scr, *, tm, tiles_per_seq):
    @pl.when(pl.program_id(0) % tiles_per_seq == 0)
    def _():
        carry_scr[...] = jnp.zeros_like(carry_scr)

    mix = jnp.concatenate([s_ref[...], r_ref[...], d_ref[...]], axis=1)
    h1 = h_ref[...] + _dot(mix, wo_ref[...])
    hn = _rms(h1, g2_ref[...]).astype(BF16)
    row8 = lax.broadcasted_iota(jnp.int32, (8, FF_CHUNK), 0)
    n_chunks = D_FF // FF_CHUNK

    def up(jj):
        return [_dot(hn, wup_ref[:, c0:c0 + FF_CHUNK]) for c0 in (jj * FF_CHUNK, D_FF + jj * FF_CHUNK)]

    def conv(u, c0):
        c6 = carry_scr[6:7, c0:c0 + FF_CHUNK]
        c7 = carry_scr[7:8, c0:c0 + FF_CHUNK]
        r1 = pltpu.roll(u, 1, 0)
        r2 = pltpu.roll(u, 2, 0)
        p1 = jnp.concatenate([jnp.where(row8 == 0, c7, r1[0:8]), r1[8:]], axis=0)
        p2 = jnp.concatenate([jnp.where(row8 == 0, c6, jnp.where(row8 == 1, c7, r2[0:8])), r2[8:]], axis=0)
        carry_scr[:, c0:c0 + FF_CHUNK] = u[tm - 8:tm, :]
        w = cw_ref[:, c0:c0 + FF_CHUNK]
        return w[0:1] * p2 + w[1:2] * p1 + w[2:3] * u + cb_ref[:, c0:c0 + FF_CHUNK]

    acc = jnp.zeros((tm, D_MODEL), F32)
    u_cur = up(0)
    for jj in range(n_chunks):
        u_next = up(jj + 1) if jj + 1 < n_chunks else None
        a = conv(u_cur[0], jj * FF_CHUNK)
        b = conv(u_cur[1], D_FF + jj * FF_CHUNK)
        act = (a * _sigmoid(a) * b).astype(BF16)
        acc = acc + _dot(act, wdn_ref[jj * FF_CHUNK:(jj + 1) * FF_CHUNK, :])
        u_cur = u_next
    o_ref[...] = h1 + acc


def mixer_mlp(h, s_out, r_out, d_out, w, seq):
    n = h.shape[0]
    tm = TM_MLP

    def rows(width):
        return pl.BlockSpec((tm, width), lambda i: (i, 0))

    consts = (w["w_out"], w["g2"], w["w_up"], w["conv_w"], w["conv_b"], w["w_down"])
    return pl.pallas_call(
        functools.partial(_mlp_kernel, tm=tm, tiles_per_seq=seq // tm),
        out_shape=jax.ShapeDtypeStruct((n, D_MODEL), F32),
        grid=(n // tm,),
        in_specs=[rows(D_MODEL), rows(SSM_WIDTH), rows(RET_WIDTH), rows(DIFF_WIDTH)]
        + [pl.BlockSpec(c.shape, lambda i: (0, 0), pipeline_mode=pl.Buffered(1)) for c in consts],
        out_specs=rows(D_MODEL),
        scratch_shapes=[pltpu.VMEM((8, 2 * D_FF), F32)],
        compiler_params=_params(("arbitrary",), 56),
        name="mixer_mlp",
    )(h, s_out, r_out, d_out, *consts)


def _ple(h_ref, p_ref, g3_ref, wpg_ref, wpe_ref):
    h2 = h_ref[...]
    gate = _sigmoid(_dot(_rms(h2, g3_ref[...]).astype(BF16), wpg_ref[...]))
    return h2 + gate * _dot(p_ref[...].astype(BF16), wpe_ref[...])


def _ple_proj_kernel(h_ref, p_ref, g3_ref, wpg_ref, wpe_ref, g1_ref, w_u, w_ret, w_dq, w_dk, w_dvt,
                     ho_ref, u_ref, ret_ref, dq_ref, dk_ref, vt_ref):
    h3 = _ple(h_ref, p_ref, g3_ref, wpg_ref, wpe_ref)
    ho_ref[...] = h3
    hn = _rms(h3, g1_ref[...]).astype(BF16)
    _project(hn, w_u, w_ret, w_dq, w_dk, w_dvt, u_ref, ret_ref, dq_ref, dk_ref, vt_ref)


def _ple_final_kernel(h_ref, p_ref, g3_ref, wpg_ref, wpe_ref, gf_ref, o_ref):
    o_ref[...] = _rms(_ple(h_ref, p_ref, g3_ref, wpg_ref, wpe_ref), gf_ref[...])


def ple_proj(h, p, w, g1_next, w_next):
    n = h.shape[0]
    tm = TM_PROJ
    shapes, specs = _proj_out(n, tm)
    consts = (w["g3"], w["w_pg"], w["w_pe"], g1_next)
    return pl.pallas_call(
        _ple_proj_kernel,
        out_shape=(jax.ShapeDtypeStruct((n, D_MODEL), F32),) + shapes,
        grid=(n // tm,),
        in_specs=[pl.BlockSpec((tm, D_MODEL), lambda i: (i, 0)), pl.BlockSpec((tm, PLE_DIM), lambda i: (i, 0))]
        + [_const_spec(c.shape) for c in consts] + _proj_weight_specs(w_next),
        out_specs=(pl.BlockSpec((tm, D_MODEL), lambda i: (i, 0)),) + specs,
        compiler_params=_params(("parallel",), 56),
        name="ple_proj",
    )(h, p, *consts, *_proj_weights(w_next))


def ple_final(h, p, w, final_g):
    n = h.shape[0]
    tm = TM_PROJ
    consts = (w["g3"], w["w_pg"], w["w_pe"], final_g)
    return pl.pallas_call(
        _ple_final_kernel,
        out_shape=jax.ShapeDtypeStruct((n, D_MODEL), F32),
        grid=(n // tm,),
        in_specs=[pl.BlockSpec((tm, D_MODEL), lambda i: (i, 0)), pl.BlockSpec((tm, PLE_DIM), lambda i: (i, 0))]
        + [_const_spec(c.shape) for c in consts],
        out_specs=pl.BlockSpec((tm, D_MODEL), lambda i: (i, 0)),
        compiler_params=_params(("parallel",), 40),
        name="ple_final",
    )(h, p, *consts)


def _layer_weights(l, norm1_g, w_in, ssm_d, ssm_w_glu, ssm_b_glu, ret_gn_g, diff_subln_g, w_out, norm2_g, w_up,
                   conv_w, conv_b, w_down, norm3_g, w_pg, w_pe):
    wi = w_in[l].astype(BF16)
    c = np.cumsum([0, SSM_WIDTH, 4 * RET_WIDTH, DIFF_WIDTH, DIFF_WIDTH, DIFF_WIDTH])
    return {
        "g1": norm1_g[l][None, :],
        "w_u": wi[:, c[0]:c[1]], "w_ret": wi[:, c[1]:c[2]], "w_dq": wi[:, c[2]:c[3]], "w_dk": wi[:, c[3]:c[4]],
        "w_dvt": wi[:, c[4]:c[5]].T,
        "ssm_d": ssm_d[l][None, :], "w_glu": ssm_w_glu[l].astype(BF16), "b_glu": ssm_b_glu[l][None, :],
        "gn": ret_gn_g[l][None, :], "subln_g": diff_subln_g[l][None, :],
        "w_out": w_out[l].astype(BF16),
        "g2": norm2_g[l][None, :], "w_up": w_up[l].astype(BF16), "conv_w": conv_w[l], "conv_b": conv_b[l][None, :],
        "w_down": w_down[l].astype(BF16),
        "g3": norm3_g[l][None, :], "w_pg": w_pg[l].astype(BF16), "w_pe": w_pe[l].astype(BF16),
    }


def kernel(x, p, norm1_g, w_in, ssm_lam_re, ssm_lam_im, ssm_log_dt, ssm_b_re, ssm_b_im, ssm_c_re, ssm_c_im, ssm_d, ssm_w_glu, ssm_b_glu, ret_gn_g, diff_lq1, diff_lk1, diff_lq2, diff_lk2, diff_subln_g, w_out, norm2_g, w_up, conv_w, conv_b, w_down, norm3_g, w_pg, w_pe, final_g):
    nb, seq, _ = x.shape
    depth = w_in.shape[0]
    n = nb * seq
    assert seq % max(TM_PROJ, RET_CHUNK, ATT_TILE, S5_CHUNK, TM_MLP) == 0 and nb == 8

    ws = [_layer_weights(l, norm1_g, w_in, ssm_d, ssm_w_glu, ssm_b_glu, ret_gn_g, diff_subln_g, w_out, norm2_g,
                         w_up, conv_w, conv_b, w_down, norm3_g, w_pg, w_pe) for l in range(depth)]
    bmat, cmat, a = s5_prep(ssm_lam_re, ssm_lam_im, ssm_log_dt, ssm_b_re, ssm_b_im, ssm_c_re, ssm_c_im)
    bmat = bmat.astype(BF16)
    cmat = cmat.astype(BF16)

    h = x.reshape(n, D_MODEL)
    u, ret, dq, dk, vt = norm_proj(h, ws[0]["g1"], ws[0])
    out = None
    for l in range(depth):
        w = ws[l]
        lambda_init = 0.8 - 0.6 * math.exp(-0.3 * l)
        s_out = s5_mixer(u.reshape(nb, seq, SSM_WIDTH), bmat[l], cmat[l], a[l], w["ssm_d"], w["w_glu"], w["b_glu"])
        r_out = retention(ret, w["gn"], nb, seq)
        d_out = diff_attention(dq, dk, vt, diff_lq1[l][None, :], diff_lk1[l][None, :], diff_lq2[l][None, :],
                               diff_lk2[l][None, :], w["subln_g"], lambda_init, nb, seq)
        h = mixer_mlp(h, s_out.reshape(n, SSM_WIDTH), r_out, d_out, w, seq)
        pl_in = p[l].reshape(n, PLE_DIM)
        if l + 1 < depth:
            h, u, ret, dq, dk, vt = ple_proj(h, pl_in, w, ws[l + 1]["g1"], ws[l + 1])
        else:
            out = ple_final(h, pl_in, w, final_g[None, :])
    return out.reshape(nb, seq, D_MODEL)
```

```python
import functools
import math

import jax
import jax.numpy as jnp
import numpy as np
from jax import lax
from jax.experimental import pallas as pl
from jax.experimental.pallas import tpu as pltpu

F32 = jnp.float32
BF16 = jnp.bfloat16

D_MODEL = 1024
PLE_DIM = 256
SSM_WIDTH = 256
SSM_GROUP = 16
SSM_GROUPS = 16
SSM_STATE = 64
SSM_STATES = SSM_GROUPS * SSM_STATE
RET_HEADS = 6
RET_HEAD_DIM = 64
RET_WIDTH = RET_HEADS * RET_HEAD_DIM
DIFF_HEADS = 6
DIFF_QK_DIM = 32
DIFF_V_DIM = 64
DIFF_WIDTH = DIFF_HEADS * DIFF_V_DIM
D_FF = 2816
EPS = 1e-6
RET_LOG_GAMMA = np.log1p(-(2.0 ** (-5.0 - np.arange(RET_HEADS)))).astype(np.float32)
ALIBI_SLOPES = (2.0 ** (-8.0 * (np.arange(DIFF_HEADS) + 1) / DIFF_HEADS)).astype(np.float32)
PROJ_COLS = np.cumsum([0, SSM_WIDTH, 4 * RET_WIDTH, DIFF_WIDTH, DIFF_WIDTH, DIFF_WIDTH])
PROJ_WIDTH = int(PROJ_COLS[-1])

LANES = 128
SUBLANES = 8
VMEM_BYTES_V7X = 64 * 1024 * 1024
NEG_BIG = -1e30
LOG2E = math.log2(math.e)

TM_PROJ = 512
TM_MLP = 256
FF_CHUNK = 256
S5_CHUNK = 128
S5_BLOCK = 16
RET_CHUNK = 256
ATT_TILE = 256
QV_TILE = 256


def _params(semantics, vmem_mib):
    assert vmem_mib * 1024 * 1024 < VMEM_BYTES_V7X
    return pltpu.CompilerParams(dimension_semantics=semantics, vmem_limit_bytes=vmem_mib * 1024 * 1024)


def _layer_spec(arr, l, single=False):
    nd = arr.ndim
    kw = {"pipeline_mode": pl.Buffered(1)} if single else {}
    return pl.BlockSpec((None,) + arr.shape[1:], lambda *_: (l,) + (0,) * (nd - 1), **kw)


def _rms(x, g):
    return x * lax.rsqrt(jnp.mean(x * x, axis=-1, keepdims=True) + EPS) * g


def _sigmoid(x):
    return 1.0 / (1.0 + jnp.exp(-x))


def _dot(a, b):
    return jnp.dot(a, b, preferred_element_type=F32)


def _dot_nt(a, b):
    return lax.dot_general(a, b, (((1,), (1,)), ((), ())), preferred_element_type=F32)


def _project(hn, w_ref, u_ref, ret_ref, dk_ref, qt_ref, vt_ref):
    c = [int(v) for v in PROJ_COLS]
    u_ref[...] = _dot(hn, w_ref[:, c[0]:c[1]])
    ret_ref[...] = _dot(hn, w_ref[:, c[1]:c[2]]).astype(BF16)
    qk = _dot(hn, w_ref[:, c[2]:c[4]])
    dk_ref[...] = qk[:, DIFF_WIDTH:].astype(BF16)
    q = qk[:, :DIFF_WIDTH] * (DIFF_QK_DIM ** -0.5 * LOG2E)
    v = _dot(hn, w_ref[:, c[4]:c[5]])
    for r in range(qt_ref.shape[0]):
        rows = slice(r * QV_TILE, (r + 1) * QV_TILE)
        qt_ref[r] = q[rows, :].T.astype(BF16)
        vt_ref[r] = v[rows, :].T.astype(BF16)


def _norm_proj_kernel(h_ref, g_ref, w_ref, u_ref, ret_ref, dk_ref, qt_ref, vt_ref):
    hn = _rms(h_ref[...], g_ref[...]).astype(BF16)
    _project(hn, w_ref, u_ref, ret_ref, dk_ref, qt_ref, vt_ref)


def _proj_out(n, tm):
    slab = jax.ShapeDtypeStruct((n // QV_TILE, DIFF_WIDTH, QV_TILE), BF16)
    shapes = (
        jax.ShapeDtypeStruct((n, SSM_WIDTH), F32),
        jax.ShapeDtypeStruct((n, 4 * RET_WIDTH), BF16),
        jax.ShapeDtypeStruct((n, DIFF_WIDTH), BF16),
        slab,
        slab,
    )
    slab_spec = pl.BlockSpec((tm // QV_TILE, DIFF_WIDTH, QV_TILE), lambda i: (i, 0, 0))
    specs = (
        pl.BlockSpec((tm, SSM_WIDTH), lambda i: (i, 0)),
        pl.BlockSpec((tm, 4 * RET_WIDTH), lambda i: (i, 0)),
        pl.BlockSpec((tm, DIFF_WIDTH), lambda i: (i, 0)),
        slab_spec,
        slab_spec,
    )
    return shapes, specs


def norm_proj(h, prm, l):
    n = h.shape[0]
    tm = TM_PROJ
    shapes, specs = _proj_out(n, tm)
    return pl.pallas_call(
        _norm_proj_kernel,
        out_shape=shapes,
        grid=(n // tm,),
        in_specs=[pl.BlockSpec((tm, D_MODEL), lambda i: (i, 0)), _layer_spec(prm["g1"], l), _layer_spec(prm["w_in"], l)],
        out_specs=specs,
        compiler_params=_params(("parallel",), 48),
        name="norm_proj",
    )(h, prm["g1"], prm["w_in"])


def _s5_prep_kernel(lr_ref, li_ref, ldt_ref, brt_ref, bit_ref, crt_ref, cit_ref, b_ref, c_ref, a_ref):
    lr = lr_ref[0]
    li = li_ref[0]
    dt = jnp.exp(ldt_ref[0])
    mag = jnp.exp(lr * dt)
    ar = mag * jnp.cos(li * dt)
    ai = mag * jnp.sin(li * dt)
    den = lr * lr + li * li
    cr = ((ar - 1.0) * lr + ai * li) / den
    ci = (ai * lr - (ar - 1.0) * li) / den
    b_ref[...] = jnp.zeros_like(b_ref)
    c_ref[...] = jnp.zeros_like(c_ref)
    for g in range(SSM_GROUPS):
        rows = slice(g * SSM_GROUP, (g + 1) * SSM_GROUP)
        cols = slice(g * SSM_STATE, (g + 1) * SSM_STATE)
        cols_im = slice(SSM_STATES + g * SSM_STATE, SSM_STATES + (g + 1) * SSM_STATE)
        crg = cr[g:g + 1, :]
        cig = ci[g:g + 1, :]
        br = brt_ref[0, g]
        bi = bit_ref[0, g]
        b_ref[0, rows, cols] = crg * br - cig * bi
        b_ref[0, rows, cols_im] = crg * bi + cig * br
        c_ref[0, cols, rows] = crt_ref[0, g]
        c_ref[0, cols_im, rows] = -cit_ref[0, g]
        a_ref[0, 0:1, cols] = ar[g:g + 1, :]
        a_ref[0, 1:2, cols] = ai[g:g + 1, :]


def s5_prep(lam_re, lam_im, log_dt, b_re, b_im, c_re, c_im):
    depth = lam_re.shape[0]
    brt = jnp.transpose(b_re, (0, 1, 3, 2))
    bit = jnp.transpose(b_im, (0, 1, 3, 2))
    crt = jnp.transpose(c_re, (0, 1, 3, 2))
    cit = jnp.transpose(c_im, (0, 1, 3, 2))
    ldt = log_dt[..., None]

    def spec(a):
        nd = a.ndim
        return pl.BlockSpec((1,) + a.shape[1:], lambda l: (l,) + (0,) * (nd - 1))

    ins = (lam_re, lam_im, ldt, brt, bit, crt, cit)
    out_shape = (
        jax.ShapeDtypeStruct((depth, SSM_WIDTH, 2 * SSM_STATES), F32),
        jax.ShapeDtypeStruct((depth, 2 * SSM_STATES, SSM_WIDTH), F32),
        jax.ShapeDtypeStruct((depth, 2, SSM_STATES), F32),
    )
    return pl.pallas_call(
        _s5_prep_kernel,
        out_shape=out_shape,
        grid=(depth,),
        in_specs=[spec(a) for a in ins],
        out_specs=tuple(spec(o) for o in out_shape),
        compiler_params=_params(("parallel",), 32),
        name="s5_prep",
    )(*ins)


def _gelu_tanh(x):
    return 0.5 * x * (1.0 + jnp.tanh(math.sqrt(2.0 / math.pi) * (x + 0.044715 * (x * x * x))))


def _s5_kernel(un_ref, up_ref, bmat_ref, cmat_ref, a_ref, d_ref, wglu_ref, bglu_ref, o_ref,
               buf_scr, st_scr, *, tc):
    nb = un_ref.shape[0]
    g = pl.program_id(0)
    n_blk = tc // S5_BLOCK
    col_blk = 2 * SSM_STATES // n_blk

    @pl.when(g == 0)
    def _():
        buf_scr[...] = jnp.zeros_like(buf_scr)

    @pl.when(g <= 1)
    def _():
        st_scr[...] = jnp.zeros_like(st_scr)

    def stages(b_in, b_scan, b_out):
        ar = jnp.broadcast_to(a_ref[0:1, :], (nb, SSM_STATES))
        ai = jnp.broadcast_to(a_ref[1:2, :], (nb, SSM_STATES))
        xr = st_scr[:, 0:SSM_STATES]
        xi = st_scr[:, SSM_STATES:2 * SSM_STATES]
        u_next = jnp.swapaxes(un_ref[...], 0, 1).reshape(tc * nb, SSM_WIDTH).astype(BF16)
        y = d_ref[...] * jnp.swapaxes(up_ref[...], 0, 1).reshape(tc * nb, SSM_WIDTH)
        for j in range(n_blk):
            cols = slice(j * col_blk, (j + 1) * col_blk)
            y = y + _dot(buf_scr[b_out, :, cols].astype(BF16), cmat_ref[cols, :])
            for t in range(j * S5_BLOCK, (j + 1) * S5_BLOCK):
                r = t * nb
                nxr = ar * xr - ai * xi + buf_scr[b_scan, r:r + nb, 0:SSM_STATES]
                nxi = ar * xi + ai * xr + buf_scr[b_scan, r:r + nb, SSM_STATES:2 * SSM_STATES]
                buf_scr[b_scan, r:r + nb, 0:SSM_STATES] = nxr
                buf_scr[b_scan, r:r + nb, SSM_STATES:2 * SSM_STATES] = nxi
                xr, xi = nxr, nxi
            buf_scr[b_in, :, cols] = _dot(u_next, bmat_ref[:, cols])
        st_scr[:, 0:SSM_STATES] = xr
        st_scr[:, SSM_STATES:2 * SSM_STATES] = xi
        z = _gelu_tanh(y)
        out = z * _sigmoid(_dot(z.astype(BF16), wglu_ref[...]) + bglu_ref[...])
        o_ref[...] = jnp.swapaxes(out.reshape(tc, nb, SSM_WIDTH), 0, 1).astype(BF16)

    for r in range(3):
        @pl.when(g % 3 == r)
        def _(r=r):
            stages(r, (r + 2) % 3, (r + 1) % 3)


def s5_mixer(u, prm, l):
    nb, seq, _ = u.shape
    tc = S5_CHUNK
    nc = seq // tc
    assert nb == SUBLANES

    def chunk(fn):
        return pl.BlockSpec((nb, tc, SSM_WIDTH), lambda g: (0, fn(g), 0))

    names = ("s5_b", "s5_c", "s5_a", "ssm_d", "w_glu", "b_glu")
    return pl.pallas_call(
        functools.partial(_s5_kernel, tc=tc),
        out_shape=jax.ShapeDtypeStruct(u.shape, BF16),
        grid=(nc + 2,),
        in_specs=[chunk(lambda g: jnp.minimum(g, nc - 1)), chunk(lambda g: jnp.clip(g - 2, 0, nc - 1))]
        + [_layer_spec(prm[k], l) for k in names],
        out_specs=chunk(lambda g: jnp.clip(g - 2, 0, nc - 1)),
        scratch_shapes=[
            pltpu.VMEM((3, tc * nb, 2 * SSM_STATES), F32),
            pltpu.VMEM((nb, 2 * SSM_STATES), F32),
        ],
        compiler_params=_params(("arbitrary",), 48),
        name="s5_mixer",
    )(u, u, *[prm[k] for k in names])


def _ret_kernel(q_ref, k_ref, v_ref, g_ref, gn_ref, o_ref, dmat_scr, decq_scr, deck_scr, sdec_scr, st_scr, *, cr):
    pairs = RET_HEADS // 2
    hd = RET_HEAD_DIM
    scale = hd ** -0.5
    lane = lax.broadcasted_iota(jnp.int32, (1, LANES), 1)
    lo = lane < hd
    rows = lax.broadcasted_iota(jnp.int32, (LANES, LANES), 0) < hd
    cols = lax.broadcasted_iota(jnp.int32, (LANES, LANES), 1) < hd
    same = rows == cols

    @pl.when((pl.program_id(0) == 0) & (pl.program_id(1) == 0))
    def _():
        t = lax.broadcasted_iota(jnp.int32, (cr, cr), 0)
        s = lax.broadcasted_iota(jnp.int32, (cr, cr), 1)
        dist = (t - s).astype(F32)
        pos = lax.broadcasted_iota(jnp.int32, (cr, LANES), 0).astype(F32)
        for h in range(RET_HEADS):
            lg = float(RET_LOG_GAMMA[h])
            dmat_scr[h // 2, :, (h % 2) * cr:(h % 2 + 1) * cr] = (
                jnp.where(dist >= 0, jnp.exp(jnp.maximum(dist, 0.0) * lg), 0.0) * scale)
        for j in range(pairs):
            lga, lgb = float(RET_LOG_GAMMA[2 * j]), float(RET_LOG_GAMMA[2 * j + 1])
            lg = jnp.where(lo, lga, lgb)
            decq_scr[j] = jnp.exp((pos + 1.0) * lg)
            deck_scr[j] = jnp.exp((cr - 1.0 - pos) * lg) * scale
            sdec_scr[j] = jnp.where(same, jnp.where(rows, math.exp(cr * lga), math.exp(cr * lgb)), 0.0)

    @pl.when(pl.program_id(1) == 0)
    def _():
        st_scr[...] = jnp.zeros_like(st_scr)

    avg2 = jnp.where(jnp.concatenate([same, same], axis=0), 1.0 / hd, 0.0).astype(BF16)
    blockmask = jnp.where(same, 1.0, 0.0).astype(F32)

    def group_mean(x):
        hi = x.astype(BF16)
        lo_part = (x - hi.astype(F32)).astype(BF16)
        return _dot(jnp.concatenate([hi, lo_part], axis=1), avg2)

    sls = [slice(j * LANES, (j + 1) * LANES) for j in range(pairs)]
    scores = []
    for j in range(pairs):
        k = k_ref[:, sls[j]]
        zero = jnp.zeros_like(k)
        k2 = jnp.concatenate([jnp.where(lo, k, zero), jnp.where(lo, zero, k)], axis=0)
        scores.append(_dot_nt(q_ref[:, sls[j]], k2))
    probs = [(scores[j] * dmat_scr[j]).astype(BF16) for j in range(pairs)]
    outs = []
    for j in range(pairs):
        q = q_ref[:, sls[j]]
        k = k_ref[:, sls[j]]
        v = v_ref[:, sls[j]]
        zero = jnp.zeros_like(v)
        v2 = jnp.concatenate([jnp.where(lo, v, zero), jnp.where(lo, zero, v)], axis=0)
        st = st_scr[j]
        cross = _dot((q.astype(F32) * decq_scr[j]).astype(BF16), st.astype(BF16))
        kdt = (k.astype(F32) * deck_scr[j]).T.astype(BF16)
        st_scr[j] = st * sdec_scr[j] + _dot(kdt, v) * blockmask
        outs.append(_dot(probs[j], v2) + cross)
    devs = [o - group_mean(o) for o in outs]
    variances = [group_mean(d * d) for d in devs]
    for j in range(pairs):
        on = devs[j] * lax.rsqrt(variances[j] + EPS) * gn_ref[:, sls[j]]
        gate = g_ref[:, sls[j]].astype(F32)
        o_ref[:, sls[j]] = (gate * _sigmoid(gate) * on).astype(BF16)


def retention(ret, prm, l, nb, seq):
    n = ret.shape[0]
    cr = RET_CHUNK
    nc = seq // cr
    pairs = RET_HEADS // 2

    def part(col):
        return pl.BlockSpec((cr, RET_WIDTH), lambda b, c: (b * nc + c, col))

    return pl.pallas_call(
        functools.partial(_ret_kernel, cr=cr),
        out_shape=jax.ShapeDtypeStruct((n, RET_WIDTH), BF16),
        grid=(nb, nc),
        in_specs=[part(0), part(1), part(2), part(3), _layer_spec(prm["gn"], l)],
        out_specs=pl.BlockSpec((cr, RET_WIDTH), lambda b, c: (b * nc + c, 0)),
        scratch_shapes=[
            pltpu.VMEM((pairs, cr, 2 * cr), F32),
            pltpu.VMEM((pairs, cr, LANES), F32),
            pltpu.VMEM((pairs, cr, LANES), F32),
            pltpu.VMEM((pairs, LANES, LANES), F32),
            pltpu.VMEM((pairs, LANES, LANES), F32),
        ],
        compiler_params=_params(("arbitrary", "arbitrary"), 32),
        name="retention",
    )(ret, ret, ret, ret, prm["gn"])


def _split_bf16(x, parts=3):
    out = []
    rem = np.float32(x)
    for _ in range(parts):
        hi = np.float32(rem.astype(BF16))
        out.append(float(hi))
        rem = np.float32(rem - hi)
    return out


def _diff_kernel(lq1_ref, lk1_ref, lq2_ref, lk2_ref, qt_ref, k_ref, vt_ref, g_ref, o_ref,
                 acc_scr, m_scr, s_scr, mt_scr, *, tile, lambda_init):
    j = pl.program_id(1)
    i = pl.program_id(2)
    hv = DIFF_V_DIM
    dq = DIFF_QK_DIM
    ones_rows = 16
    combos = 4
    n_split = 3

    lam = (jnp.exp(jnp.sum(lq1_ref[...] * lk1_ref[...], axis=-1, keepdims=True))
           - jnp.exp(jnp.sum(lq2_ref[...] * lk2_ref[...], axis=-1, keepdims=True)) + lambda_init)

    def pick(vals):
        return jnp.where(j == 0, vals[0], jnp.where(j == 1, vals[1], vals[2]))

    slope2 = [float(np.float32(ALIBI_SLOPES[h]) * np.float32(LOG2E)) for h in range(DIFF_HEADS)]
    parts = [_split_bf16(s, n_split) for s in slope2]
    slopes = [pick([slope2[2 * jj + hl] for jj in range(3)]) for hl in range(2)]
    pieces = [[pick([parts[2 * jj + hl][n] for jj in range(3)]) for n in range(n_split)] for hl in range(2)]

    s_loc = lax.broadcasted_iota(jnp.int32, (tile, combos * tile), 0)
    t_loc = lax.broadcasted_iota(jnp.int32, (tile, combos * tile), 1) & (tile - 1)
    causal4 = s_loc <= t_loc
    acc_scr[...] = jnp.zeros_like(acc_scr)
    m_scr[...] = jnp.full_like(m_scr, NEG_BIG)

    qt = qt_ref[...]
    def own_rows(c):
        pieces_c = [jnp.zeros((dq * c, tile), BF16), qt[dq * c:dq * (c + 1)],
                    jnp.zeros((LANES - dq * (c + 1), tile), BF16)]
        return jnp.concatenate([x for x in pieces_c if x.shape[0]], axis=0)

    qstack = jnp.concatenate([own_rows(c) for c in range(combos)], axis=1)
    row_t = lax.broadcasted_iota(jnp.int32, (LANES, tile), 0)
    lane_t = lax.broadcasted_iota(jnp.int32, (tile, LANES), 1)
    qfeat = []
    for hl in range(2):
        f = jnp.zeros((LANES, tile), F32)
        for n in range(n_split):
            f = jnp.where(row_t == n, pieces[hl][n], f)
        qfeat += [f.astype(BF16)] * 2
    qt_ext = jnp.concatenate([qstack, jnp.concatenate(qfeat, axis=1)], axis=0)
    pos = lax.broadcasted_iota(jnp.int32, (tile, LANES), 0).astype(F32)
    kfeat = jnp.where(lane_t < n_split, pos, 0.0).astype(BF16)
    ones = jnp.ones((ones_rows, tile), BF16)

    def score(kvt, buf, diag=False):
        kv0 = pl.multiple_of(kvt * tile, tile)
        k_ext = jnp.concatenate([k_ref[pl.ds(kv0, tile), :], kfeat], axis=1)
        s_all = _dot(k_ext, qt_ext)
        if diag:
            s_all = jnp.where(causal4, s_all, NEG_BIG)
        s_scr[buf] = s_all
        for c in range(combos):
            mt_scr[buf, c] = jnp.max(s_all[:, c * tile:(c + 1) * tile], axis=0, keepdims=True)

    def softmax(buf, kvt):
        off = ((kvt - i) * tile).astype(F32)
        ps, alphas = [], []
        for c in range(combos):
            shift = slopes[c // 2] * off
            m_old = m_scr[c]
            m_new = jnp.maximum(m_old, mt_scr[buf, c] + shift)
            ps.append(jnp.exp2(s_scr[buf, :, c * tile:(c + 1) * tile] - (m_new - shift)).astype(BF16))
            alphas.append(jnp.exp2(m_old - m_new))
            m_scr[c] = m_new
        return ps, alphas

    def accumulate(kvt, ps, alphas):
        vt = vt_ref[kvt]
        for hl in range(2):
            vaug = jnp.concatenate([vt[hl * hv:(hl + 1) * hv, :], ones], axis=0)
            pv = _dot(vaug, jnp.concatenate(ps[2 * hl:2 * hl + 2], axis=1))
            acc_scr[hl] = acc_scr[hl] * jnp.concatenate(alphas[2 * hl:2 * hl + 2], axis=1) + pv

    def consume(buf, kvt):
        accumulate(kvt, *softmax(buf, kvt))

    score(i, 1, diag=True)
    score(0, 0)
    consume(1, i)

    def body(u, carry):
        t = 2 * u
        pa = softmax(0, t)
        score(t + 1, 1)
        accumulate(t, *pa)
        pb = softmax(1, t + 1)
        score(t + 2, 0)
        accumulate(t + 1, *pb)
        return carry

    lax.fori_loop(0, lax.shift_right_logical(jnp.maximum(i - 1, 0), 1), body, 0)

    @pl.when((i & 1) == 1)
    def _():
        consume(0, i - 1)

    @pl.when(((i & 1) == 0) & (i > 0))
    def _():
        score(i - 1, 1)
        consume(0, i - 2)
        consume(1, i - 1)

    outs = []
    for hl in range(2):
        a0 = acc_scr[hl, :, 0:tile]
        a1 = acc_scr[hl, :, tile:2 * tile]
        o = a0[0:hv] / a0[hv:hv + 1] - lam * (a1[0:hv] / a1[hv:hv + 1])
        outs.append(o * lax.rsqrt(jnp.mean(o * o, axis=0, keepdims=True) + EPS))
    ot = jnp.concatenate(outs, axis=0)
    o_ref[...] = (ot.T * g_ref[...] * (1.0 - lambda_init)).astype(BF16)


def diff_attention(qt, dk, vt, prm, l, lambda_init, nb, seq):
    n = dk.shape[0]
    tile = ATT_TILE
    assert tile == QV_TILE
    nq = seq // tile
    pairs = DIFF_HEADS // 2
    lam_names = ("lq1", "lk1", "lq2", "lk2")
    return pl.pallas_call(
        functools.partial(_diff_kernel, tile=tile, lambda_init=lambda_init),
        out_shape=jax.ShapeDtypeStruct((n, DIFF_WIDTH), BF16),
        grid=(nb, pairs, nq),
        in_specs=[_layer_spec(prm[k], l) for k in lam_names] + [
            pl.BlockSpec((None, LANES, tile), lambda b, j, i: (b * nq + i, j, 0)),
            pl.BlockSpec((seq, LANES), lambda b, j, i: (b, j)),
            pl.BlockSpec((nq, LANES, tile), lambda b, j, i: (b, j, 0)),
            pl.BlockSpec((None, 1, LANES), lambda b, j, i: (l, 0, j)),
        ],
        out_specs=pl.BlockSpec((tile, LANES), lambda b, j, i: (b * nq + i, j)),
        scratch_shapes=[
            pltpu.VMEM((2, DIFF_V_DIM + 16, 2 * tile), F32),
            pltpu.VMEM((4, 1, tile), F32),
            pltpu.VMEM((2, tile, 4 * tile), F32),
            pltpu.VMEM((2, 4, 1, tile), F32),
        ],
        compiler_params=_params(("parallel", "parallel", "arbitrary"), 32),
        name="diff_attention",
    )(*[prm[k] for k in lam_names], qt, dk, vt, prm["subln"])


def _mlp_kernel(h_ref, s_ref, r_ref, d_ref, wo_ref, g2_ref, wup_ref, cw_ref, cb_ref, wdn_ref, o_ref,
                carry_scr, *, tm, tiles_per_seq):
    @pl.when(pl.program_id(0) % tiles_per_seq == 0)
    def _():
        carry_scr[...] = jnp.zeros_like(carry_scr)

    mix = jnp.concatenate([s_ref[...], r_ref[...], d_ref[...]], axis=1)
    h1 = h_ref[...] + _dot(mix, wo_ref[...])
    hn = _rms(h1, g2_ref[...]).astype(BF16)
    row8 = lax.broadcasted_iota(jnp.int32, (SUBLANES, FF_CHUNK), 0)
    n_chunks = D_FF // FF_CHUNK

    def up(jj):
        return [_dot(hn, wup_ref[:, c0:c0 + FF_CHUNK]) for c0 in (jj * FF_CHUNK, D_FF + jj * FF_CHUNK)]

    def conv(u, c0):
        c6 = carry_scr[6:7, c0:c0 + FF_CHUNK]
        c7 = carry_scr[7:8, c0:c0 + FF_CHUNK]
        r1 = pltpu.roll(u, 1, 0)
        r2 = pltpu.roll(u, 2, 0)
        p1 = jnp.concatenate([jnp.where(row8 == 0, c7, r1[0:8]), r1[8:]], axis=0)
        p2 = jnp.concatenate([jnp.where(row8 == 0, c6, jnp.where(row8 == 1, c7, r2[0:8])), r2[8:]], axis=0)
        carry_scr[:, c0:c0 + FF_CHUNK] = u[tm - 8:tm, :]
        w = cw_ref[:, c0:c0 + FF_CHUNK]
        return w[0:1] * p2 + w[1:2] * p1 + w[2:3] * u + cb_ref[:, c0:c0 + FF_CHUNK]

    acc = jnp.zeros((tm, D_MODEL), F32)
    u_cur = up(0)
    for jj in range(n_chunks):
        u_next = up(jj + 1) if jj + 1 < n_chunks else None
        a = conv(u_cur[0], jj * FF_CHUNK)
        b = conv(u_cur[1], D_FF + jj * FF_CHUNK)
        act = (a * _sigmoid(a) * b).astype(BF16)
        acc = acc + _dot(act, wdn_ref[jj * FF_CHUNK:(jj + 1) * FF_CHUNK, :])
        u_cur = u_next
    o_ref[...] = h1 + acc


def mixer_mlp(h, s_out, r_out, d_out, prm, l, seq):
    n = h.shape[0]
    tm = TM_MLP

    def rows_in(width):
        return pl.BlockSpec((tm, width), lambda i: (i, 0))

    names = ("w_out", "g2", "w_up", "conv_w", "conv_b", "w_down")
    return pl.pallas_call(
        functools.partial(_mlp_kernel, tm=tm, tiles_per_seq=seq // tm),
        out_shape=jax.ShapeDtypeStruct((n, D_MODEL), F32),
        grid=(n // tm,),
        in_specs=[rows_in(D_MODEL), rows_in(SSM_WIDTH), rows_in(RET_WIDTH), rows_in(DIFF_WIDTH)]
        + [_layer_spec(prm[k], l, single=True) for k in names],
        out_specs=rows_in(D_MODEL),
        scratch_shapes=[pltpu.VMEM((SUBLANES, 2 * D_FF), F32)],
        compiler_params=_params(("arbitrary",), 56),
        name="mixer_mlp",
    )(h, s_out, r_out, d_out, *[prm[k] for k in names])


def _ple(h_ref, p_ref, g3_ref, wpg_ref, wpe_ref):
    h2 = h_ref[...]
    gate = _sigmoid(_dot(_rms(h2, g3_ref[...]).astype(BF16), wpg_ref[...]))
    return h2 + gate * _dot(p_ref[...].astype(BF16), wpe_ref[...])


def _ple_proj_kernel(h_ref, p_ref, g3_ref, wpg_ref, wpe_ref, g1_ref, w_ref,
                     ho_ref, u_ref, ret_ref, dk_ref, qt_ref, vt_ref):
    h3 = _ple(h_ref, p_ref, g3_ref, wpg_ref, wpe_ref)
    ho_ref[...] = h3
    hn = _rms(h3, g1_ref[...]).astype(BF16)
    _project(hn, w_ref, u_ref, ret_ref, dk_ref, qt_ref, vt_ref)


def _ple_final_kernel(h_ref, p_ref, g3_ref, wpg_ref, wpe_ref, gf_ref, o_ref):
    o_ref[...] = _rms(_ple(h_ref, p_ref, g3_ref, wpg_ref, wpe_ref), gf_ref[...])


def _ple_in_specs(n, tm, l, prm):
    return [pl.BlockSpec((tm, D_MODEL), lambda i: (i, 0)), pl.BlockSpec((tm, PLE_DIM), lambda i: (l * (n // tm) + i, 0)),
            _layer_spec(prm["g3"], l), _layer_spec(prm["w_pg"], l), _layer_spec(prm["w_pe"], l)]


def ple_proj(h, p_flat, prm, l):
    n = h.shape[0]
    tm = TM_PROJ
    shapes, specs = _proj_out(n, tm)
    return pl.pallas_call(
        _ple_proj_kernel,
        out_shape=(jax.ShapeDtypeStruct((n, D_MODEL), F32),) + shapes,
        grid=(n // tm,),
        in_specs=_ple_in_specs(n, tm, l, prm) + [_layer_spec(prm["g1"], l + 1), _layer_spec(prm["w_in"], l + 1)],
        out_specs=(pl.BlockSpec((tm, D_MODEL), lambda i: (i, 0)),) + specs,
        compiler_params=_params(("parallel",), 56),
        name="ple_proj",
    )(h, p_flat, prm["g3"], prm["w_pg"], prm["w_pe"], prm["g1"], prm["w_in"])


def ple_final(h, p_flat, prm, l, final_g):
    n = h.shape[0]
    tm = TM_PROJ
    return pl.pallas_call(
        _ple_final_kernel,
        out_shape=jax.ShapeDtypeStruct((n, D_MODEL), F32),
        grid=(n // tm,),
        in_specs=_ple_in_specs(n, tm, l, prm) + [pl.BlockSpec(final_g.shape, lambda i: (0, 0))],
        out_specs=pl.BlockSpec((tm, D_MODEL), lambda i: (i, 0)),
        compiler_params=_params(("parallel",), 40),
        name="ple_final",
    )(h, p_flat, prm["g3"], prm["w_pg"], prm["w_pe"], final_g)


def kernel(x, p, norm1_g, w_in, ssm_lam_re, ssm_lam_im, ssm_log_dt, ssm_b_re, ssm_b_im, ssm_c_re, ssm_c_im, ssm_d, ssm_w_glu, ssm_b_glu, ret_gn_g, diff_lq1, diff_lk1, diff_lq2, diff_lk2, diff_subln_g, w_out, norm2_g, w_up, conv_w, conv_b, w_down, norm3_g, w_pg, w_pe, final_g):
    nb, seq, _ = x.shape
    depth = w_in.shape[0]
    n = nb * seq
    assert seq % max(TM_PROJ, RET_CHUNK, ATT_TILE, S5_CHUNK, TM_MLP) == 0 and w_in.shape[2] == PROJ_WIDTH

    def row(a):
        return a[:, None, :]

    bmat, cmat, s5_a = s5_prep(ssm_lam_re, ssm_lam_im, ssm_log_dt, ssm_b_re, ssm_b_im, ssm_c_re, ssm_c_im)
    prm = {
        "w_in": w_in.astype(BF16), "w_out": w_out.astype(BF16), "w_up": w_up.astype(BF16),
        "w_down": w_down.astype(BF16), "w_pg": w_pg.astype(BF16), "w_pe": w_pe.astype(BF16),
        "w_glu": ssm_w_glu.astype(BF16), "s5_b": bmat.astype(BF16), "s5_c": cmat.astype(BF16), "s5_a": s5_a,
        "g1": row(norm1_g), "g2": row(norm2_g), "g3": row(norm3_g), "ssm_d": row(ssm_d), "b_glu": row(ssm_b_glu),
        "gn": row(ret_gn_g), "subln": row(diff_subln_g), "conv_w": conv_w, "conv_b": row(conv_b),
        "lq1": row(diff_lq1), "lk1": row(diff_lk1), "lq2": row(diff_lq2), "lk2": row(diff_lk2),
    }
    p_flat = p.reshape(depth * n, PLE_DIM)

    h = x.reshape(n, D_MODEL)
    u, ret, dk, qt, vt = norm_proj(h, prm, 0)
    out = None
    for l in range(depth):
        lambda_init = 0.8 - 0.6 * math.exp(-0.3 * l)
        s_out = s5_mixer(u.reshape(nb, seq, SSM_WIDTH), prm, l)
        r_out = retention(ret, prm, l, nb, seq)
        d_out = diff_attention(qt, dk, vt, prm, l, lambda_init, nb, seq)
        h = mixer_mlp(h, s_out.reshape(n, SSM_WIDTH), r_out, d_out, prm, l, seq)
        if l + 1 < depth:
            h, u, ret, dk, qt, vt = ple_proj(h, p_flat, prm, l)
        else:
            out = ple_final(h, p_flat, prm, l, final_g[None, :])
    return out.reshape(nb, seq, D_MODEL)
```

```python
import functools
import math

import jax
import jax.numpy as jnp
import numpy as np
from jax import lax
from jax.experimental import pallas as pl
from jax.experimental.pallas import tpu as pltpu

F32 = jnp.float32
BF16 = jnp.bfloat16

D_MODEL = 1024
PLE_DIM = 256
SSM_WIDTH = 256
SSM_GROUP = 16
SSM_GROUPS = 16
SSM_STATE = 64
SSM_STATES = SSM_GROUPS * SSM_STATE
RET_HEADS = 6
RET_HEAD_DIM = 64
RET_WIDTH = RET_HEADS * RET_HEAD_DIM
DIFF_HEADS = 6
DIFF_QK_DIM = 32
DIFF_V_DIM = 64
DIFF_WIDTH = DIFF_HEADS * DIFF_V_DIM
D_FF = 2816
EPS = 1e-6
RET_LOG_GAMMA = np.log1p(-(2.0 ** (-5.0 - np.arange(RET_HEADS)))).astype(np.float32)
ALIBI_SLOPES = (2.0 ** (-8.0 * (np.arange(DIFF_HEADS) + 1) / DIFF_HEADS)).astype(np.float32)
PROJ_COLS = np.cumsum([0, SSM_WIDTH, 4 * RET_WIDTH, DIFF_WIDTH, DIFF_WIDTH, DIFF_WIDTH])
PROJ_WIDTH = int(PROJ_COLS[-1])

LANES = 128
SUBLANES = 8
VMEM_BYTES_V7X = 64 * 1024 * 1024
NEG_BIG = -1e30
LOG2E = math.log2(math.e)

TM_PROJ = 512
TM_MLP = 256
FF_CHUNK = 256
DOWN_GROUP = 4
S5_CHUNK = 128
S5_BLOCK = 16
RET_CHUNK = 256
ATT_TILE = 256
QV_TILE = 256


def _params(semantics, vmem_mib):
    assert vmem_mib * 1024 * 1024 < VMEM_BYTES_V7X
    return pltpu.CompilerParams(dimension_semantics=semantics, vmem_limit_bytes=vmem_mib * 1024 * 1024)


def _layer_spec(arr, l, single=False):
    nd = arr.ndim
    kw = {"pipeline_mode": pl.Buffered(1)} if single else {}
    return pl.BlockSpec((None,) + arr.shape[1:], lambda *_: (l,) + (0,) * (nd - 1), **kw)


def _rms(x, g):
    return x * lax.rsqrt(jnp.mean(x * x, axis=-1, keepdims=True) + EPS) * g


def _sigmoid(x):
    return 1.0 / (1.0 + jnp.exp(-x))


def _dot(a, b):
    return jnp.dot(a, b, preferred_element_type=F32)


def _dot_nt(a, b):
    return lax.dot_general(a, b, (((1,), (1,)), ((), ())), preferred_element_type=F32)


def _project(hn, w_ref, u_ref, ret_ref, dk_ref, qt_ref, vt_ref):
    c = [int(v) for v in PROJ_COLS]
    u_ref[...] = _dot(hn, w_ref[:, c[0]:c[1]])
    ret_ref[...] = _dot(hn, w_ref[:, c[1]:c[2]]).astype(BF16)
    qk = _dot(hn, w_ref[:, c[2]:c[4]])
    dk_ref[...] = qk[:, DIFF_WIDTH:].astype(BF16)
    q = qk[:, :DIFF_WIDTH] * (DIFF_QK_DIM ** -0.5 * LOG2E)
    v = _dot(hn, w_ref[:, c[4]:c[5]])
    for r in range(qt_ref.shape[0]):
        rows = slice(r * QV_TILE, (r + 1) * QV_TILE)
        qt_ref[r] = q[rows, :].T.astype(BF16)
        vt_ref[r] = v[rows, :].T.astype(BF16)


def _norm_proj_kernel(h_ref, g_ref, w_ref, u_ref, ret_ref, dk_ref, qt_ref, vt_ref):
    hn = _rms(h_ref[...], g_ref[...]).astype(BF16)
    _project(hn, w_ref, u_ref, ret_ref, dk_ref, qt_ref, vt_ref)


def _proj_out(n, tm):
    slab = jax.ShapeDtypeStruct((n // QV_TILE, DIFF_WIDTH, QV_TILE), BF16)
    shapes = (
        jax.ShapeDtypeStruct((n, SSM_WIDTH), F32),
        jax.ShapeDtypeStruct((n, 4 * RET_WIDTH), BF16),
        jax.ShapeDtypeStruct((n, DIFF_WIDTH), BF16),
        slab,
        slab,
    )
    slab_spec = pl.BlockSpec((tm // QV_TILE, DIFF_WIDTH, QV_TILE), lambda i: (i, 0, 0))
    specs = (
        pl.BlockSpec((tm, SSM_WIDTH), lambda i: (i, 0)),
        pl.BlockSpec((tm, 4 * RET_WIDTH), lambda i: (i, 0)),
        pl.BlockSpec((tm, DIFF_WIDTH), lambda i: (i, 0)),
        slab_spec,
        slab_spec,
    )
    return shapes, specs


def norm_proj(h, prm, l):
    n = h.shape[0]
    tm = TM_PROJ
    shapes, specs = _proj_out(n, tm)
    return pl.pallas_call(
        _norm_proj_kernel,
        out_shape=shapes,
        grid=(n // tm,),
        in_specs=[pl.BlockSpec((tm, D_MODEL), lambda i: (i, 0)), _layer_spec(prm["g1"], l), _layer_spec(prm["w_in"], l)],
        out_specs=specs,
        compiler_params=_params(("parallel",), 48),
        name="norm_proj",
    )(h, prm["g1"], prm["w_in"])


def _s5_prep_kernel(lr_ref, li_ref, ldt_ref, brt_ref, bit_ref, crt_ref, cit_ref, b_ref, c_ref, a_ref):
    lr = lr_ref[0]
    li = li_ref[0]
    dt = jnp.exp(ldt_ref[0])
    mag = jnp.exp(lr * dt)
    ar = mag * jnp.cos(li * dt)
    ai = mag * jnp.sin(li * dt)
    den = lr * lr + li * li
    cr = ((ar - 1.0) * lr + ai * li) / den
    ci = (ai * lr - (ar - 1.0) * li) / den
    b_ref[...] = jnp.zeros_like(b_ref)
    c_ref[...] = jnp.zeros_like(c_ref)
    for g in range(SSM_GROUPS):
        rows = slice(g * SSM_GROUP, (g + 1) * SSM_GROUP)
        cols = slice(g * SSM_STATE, (g + 1) * SSM_STATE)
        cols_im = slice(SSM_STATES + g * SSM_STATE, SSM_STATES + (g + 1) * SSM_STATE)
        crg = cr[g:g + 1, :]
        cig = ci[g:g + 1, :]
        br = brt_ref[0, g]
        bi = bit_ref[0, g]
        b_ref[0, rows, cols] = crg * br - cig * bi
        b_ref[0, rows, cols_im] = crg * bi + cig * br
        c_ref[0, cols, rows] = crt_ref[0, g]
        c_ref[0, cols_im, rows] = -cit_ref[0, g]
        a_ref[0, 0:1, cols] = ar[g:g + 1, :]
        a_ref[0, 1:2, cols] = ai[g:g + 1, :]


def s5_prep(lam_re, lam_im, log_dt, b_re, b_im, c_re, c_im):
    depth = lam_re.shape[0]
    brt = jnp.transpose(b_re, (0, 1, 3, 2))
    bit = jnp.transpose(b_im, (0, 1, 3, 2))
    crt = jnp.transpose(c_re, (0, 1, 3, 2))
    cit = jnp.transpose(c_im, (0, 1, 3, 2))
    ldt = log_dt[..., None]

    def spec(a):
        nd = a.ndim
        return pl.BlockSpec((1,) + a.shape[1:], lambda l: (l,) + (0,) * (nd - 1))

    ins = (lam_re, lam_im, ldt, brt, bit, crt, cit)
    out_shape = (
        jax.ShapeDtypeStruct((depth, SSM_WIDTH, 2 * SSM_STATES), F32),
        jax.ShapeDtypeStruct((depth, 2 * SSM_STATES, SSM_WIDTH), F32),
        jax.ShapeDtypeStruct((depth, 2, SSM_STATES), F32),
    )
    return pl.pallas_call(
        _s5_prep_kernel,
        out_shape=out_shape,
        grid=(depth,),
        in_specs=[spec(a) for a in ins],
        out_specs=tuple(spec(o) for o in out_shape),
        compiler_params=_params(("parallel",), 32),
        name="s5_prep",
    )(*ins)


def _gelu_tanh(x):
    return 0.5 * x * (1.0 + jnp.tanh(math.sqrt(2.0 / math.pi) * (x + 0.044715 * (x * x * x))))


def _s5_kernel(un_ref, up_ref, bmat_ref, cmat_ref, a_ref, d_ref, wglu_ref, bglu_ref, o_ref,
               buf_scr, st_scr, *, tc):
    nb = un_ref.shape[0]
    g = pl.program_id(0)
    n_blk = tc // S5_BLOCK
    col_blk = 2 * SSM_STATES // n_blk

    @pl.when(g == 0)
    def _():
        buf_scr[...] = jnp.zeros_like(buf_scr)

    @pl.when(g <= 1)
    def _():
        st_scr[...] = jnp.zeros_like(st_scr)

    def stages(b_in, b_scan, b_out):
        ar = jnp.broadcast_to(a_ref[0:1, :], (nb, SSM_STATES))
        ai = jnp.broadcast_to(a_ref[1:2, :], (nb, SSM_STATES))
        xr = st_scr[:, 0:SSM_STATES]
        xi = st_scr[:, SSM_STATES:2 * SSM_STATES]
        u_next = jnp.swapaxes(un_ref[...], 0, 1).reshape(tc * nb, SSM_WIDTH).astype(BF16)
        y = d_ref[...] * jnp.swapaxes(up_ref[...], 0, 1).reshape(tc * nb, SSM_WIDTH)
        for j in range(n_blk):
            cols = slice(j * col_blk, (j + 1) * col_blk)
            y = y + _dot(buf_scr[b_out, :, cols].astype(BF16), cmat_ref[cols, :])
            for t in range(j * S5_BLOCK, (j + 1) * S5_BLOCK):
                r = t * nb
                nxr = ar * xr - ai * xi + buf_scr[b_scan, r:r + nb, 0:SSM_STATES]
                nxi = ar * xi + ai * xr + buf_scr[b_scan, r:r + nb, SSM_STATES:2 * SSM_STATES]
                buf_scr[b_scan, r:r + nb, 0:SSM_STATES] = nxr
                buf_scr[b_scan, r:r + nb, SSM_STATES:2 * SSM_STATES] = nxi
                xr, xi = nxr, nxi
            buf_scr[b_in, :, cols] = _dot(u_next, bmat_ref[:, cols])
        st_scr[:, 0:SSM_STATES] = xr
        st_scr[:, SSM_STATES:2 * SSM_STATES] = xi
        z = _gelu_tanh(y)
        out = z * _sigmoid(_dot(z.astype(BF16), wglu_ref[...]) + bglu_ref[...])
        o_ref[...] = jnp.swapaxes(out.reshape(tc, nb, SSM_WIDTH), 0, 1).astype(BF16)

    for r in range(3):
        @pl.when(g % 3 == r)
        def _(r=r):
            stages(r, (r + 2) % 3, (r + 1) % 3)


def s5_mixer(u, prm, l):
    nb, seq, _ = u.shape
    tc = S5_CHUNK
    nc = seq // tc
    assert nb == SUBLANES

    def chunk(fn):
        return pl.BlockSpec((nb, tc, SSM_WIDTH), lambda g: (0, fn(g), 0))

    names = ("s5_b", "s5_c", "s5_a", "ssm_d", "w_glu", "b_glu")
    return pl.pallas_call(
        functools.partial(_s5_kernel, tc=tc),
        out_shape=jax.ShapeDtypeStruct(u.shape, BF16),
        grid=(nc + 2,),
        in_specs=[chunk(lambda g: jnp.minimum(g, nc - 1)), chunk(lambda g: jnp.clip(g - 2, 0, nc - 1))]
        + [_layer_spec(prm[k], l) for k in names],
        out_specs=chunk(lambda g: jnp.clip(g - 2, 0, nc - 1)),
        scratch_shapes=[
            pltpu.VMEM((3, tc * nb, 2 * SSM_STATES), F32),
            pltpu.VMEM((nb, 2 * SSM_STATES), F32),
        ],
        compiler_params=_params(("arbitrary",), 48),
        name="s5_mixer",
    )(u, u, *[prm[k] for k in names])


def _ret_kernel(q_ref, k_ref, v_ref, g_ref, gn_ref, o_ref, dmat_scr, decq_scr, deck_scr, sdec_scr, st_scr, *, cr):
    pairs = RET_HEADS // 2
    hd = RET_HEAD_DIM
    scale = hd ** -0.5
    lane = lax.broadcasted_iota(jnp.int32, (1, LANES), 1)
    lo = lane < hd
    rows = lax.broadcasted_iota(jnp.int32, (LANES, LANES), 0) < hd
    cols = lax.broadcasted_iota(jnp.int32, (LANES, LANES), 1) < hd
    same = rows == cols

    @pl.when(pl.program_id(0) == 0)
    def _():
        t = lax.broadcasted_iota(jnp.int32, (cr, cr), 0)
        s = lax.broadcasted_iota(jnp.int32, (cr, cr), 1)
        dist = (t - s).astype(F32)
        pos = lax.broadcasted_iota(jnp.int32, (cr, LANES), 0).astype(F32)
        for h in range(RET_HEADS):
            lg = float(RET_LOG_GAMMA[h])
            dmat_scr[h // 2, :, (h % 2) * cr:(h % 2 + 1) * cr] = (
                jnp.where(dist >= 0, jnp.exp(jnp.maximum(dist, 0.0) * lg), 0.0) * scale)
        for j in range(pairs):
            lga, lgb = float(RET_LOG_GAMMA[2 * j]), float(RET_LOG_GAMMA[2 * j + 1])
            lg = jnp.where(lo, lga, lgb)
            decq_scr[j] = jnp.exp((pos + 1.0) * lg)
            deck_scr[j] = jnp.exp((cr - 1.0 - pos) * lg) * scale
            sdec_scr[j] = jnp.where(same, jnp.where(rows, math.exp(cr * lga), math.exp(cr * lgb)), 0.0)

    st_scr[...] = jnp.zeros_like(st_scr)

    avg2 =jnp.where(jnp.concatenate([same, same], axis=0), 1.0 / hd, 0.0).astype(BF16)
    blockmask = jnp.where(same, 1.0, 0.0).astype(F32)

    def group_mean(x):
        hi = x.astype(BF16)
        lo_part = (x - hi.astype(F32)).astype(BF16)
        return _dot(jnp.concatenate([hi, lo_part], axis=1), avg2)

    sls = [slice(j * LANES, (j + 1) * LANES) for j in range(pairs)]

    def chunk(c, carry):
        rows = pl.ds(pl.multiple_of(c * cr, cr), cr)
        scores = []
        for j in range(pairs):
            k = k_ref[rows, sls[j]]
            zero = jnp.zeros_like(k)
            k2 = jnp.concatenate([jnp.where(lo, k, zero), jnp.where(lo, zero, k)], axis=0)
            scores.append(_dot_nt(q_ref[rows, sls[j]], k2))
        probs = [(scores[j] * dmat_scr[j]).astype(BF16) for j in range(pairs)]
        outs = []
        for j in range(pairs):
            q = q_ref[rows, sls[j]]
            k = k_ref[rows, sls[j]]
            v = v_ref[rows, sls[j]]
            zero = jnp.zeros_like(v)
            v2 = jnp.concatenate([jnp.where(lo, v, zero), jnp.where(lo, zero, v)], axis=0)
            st = st_scr[j]
            cross = _dot((q.astype(F32) * decq_scr[j]).astype(BF16), st.astype(BF16))
            kdt = (k.astype(F32) * deck_scr[j]).T.astype(BF16)
            st_scr[j] = st * sdec_scr[j] + _dot(kdt, v) * blockmask
            outs.append(_dot(probs[j], v2) + cross)
        devs = [o - group_mean(o) for o in outs]
        variances = [group_mean(d * d) for d in devs]
        for j in range(pairs):
            on = devs[j] * lax.rsqrt(variances[j] + EPS) * gn_ref[:, sls[j]]
            gate = g_ref[rows, sls[j]].astype(F32)
            o_ref[rows, sls[j]] = (gate * _sigmoid(gate) * on).astype(BF16)
        return carry

    lax.fori_loop(0, q_ref.shape[0] // cr, chunk, 0)


def retention(ret, prm, l, nb, seq):
    n = ret.shape[0]
    cr = RET_CHUNK
    pairs = RET_HEADS // 2

    def part(col):
        return pl.BlockSpec((seq, RET_WIDTH), lambda b: (b, col))

    return pl.pallas_call(
        functools.partial(_ret_kernel, cr=cr),
        out_shape=jax.ShapeDtypeStruct((n, RET_WIDTH), BF16),
        grid=(nb,),
        in_specs=[part(0), part(1), part(2), part(3), _layer_spec(prm["gn"], l)],
        out_specs=pl.BlockSpec((seq, RET_WIDTH), lambda b: (b, 0)),
        scratch_shapes=[
            pltpu.VMEM((pairs, cr, 2 * cr), F32),
            pltpu.VMEM((pairs, cr, LANES), F32),
            pltpu.VMEM((pairs, cr, LANES), F32),
            pltpu.VMEM((pairs, LANES, LANES), F32),
            pltpu.VMEM((pairs, LANES, LANES), F32),
        ],
        compiler_params=_params(("arbitrary",), 40),
        name="retention",
    )(ret, ret, ret, ret, prm["gn"])


def _split_bf16(x, parts=3):
    out = []
    rem = np.float32(x)
    for _ in range(parts):
        hi = np.float32(rem.astype(BF16))
        out.append(float(hi))
        rem = np.float32(rem - hi)
    return out


def _diff_kernel(lq1_ref, lk1_ref, lq2_ref, lk2_ref, qt_ref, k_ref, vt_ref, g_ref, o_ref,
                 acc_scr, m_scr, s_scr, mt_scr, qx_scr, *, tile, lambda_init):
    j = pl.program_id(1)
    nq = qt_ref.shape[0]
    hv = DIFF_V_DIM
    dq = DIFF_QK_DIM
    ones_rows = 16
    combos = 4
    n_split = 3

    lam = (jnp.exp(jnp.sum(lq1_ref[...] * lk1_ref[...], axis=-1, keepdims=True))
           - jnp.exp(jnp.sum(lq2_ref[...] * lk2_ref[...], axis=-1, keepdims=True)) + lambda_init)

    def pick(vals):
        return jnp.where(j == 0, vals[0], jnp.where(j == 1, vals[1], vals[2]))

    slope2 = [float(np.float32(ALIBI_SLOPES[h]) * np.float32(LOG2E)) for h in range(DIFF_HEADS)]
    parts = [_split_bf16(s, n_split) for s in slope2]
    slopes = [pick([slope2[2 * jj + hl] for jj in range(3)]) for hl in range(2)]
    pieces = [[pick([parts[2 * jj + hl][n] for jj in range(3)]) for n in range(n_split)] for hl in range(2)]

    s_loc = lax.broadcasted_iota(jnp.int32, (tile, combos * tile), 0)
    t_loc = lax.broadcasted_iota(jnp.int32, (tile, combos * tile), 1) & (tile - 1)
    causal4 = s_loc <= t_loc

    row_t = lax.broadcasted_iota(jnp.int32, (LANES, tile), 0)
    lane_t = lax.broadcasted_iota(jnp.int32, (tile, LANES), 1)
    qfeat = []
    for hl in range(2):
        f = jnp.zeros((LANES, tile), F32)
        for n in range(n_split):
            f = jnp.where(row_t == n, pieces[hl][n], f)
        qfeat += [f.astype(BF16)] * 2
    qfeat = jnp.concatenate(qfeat, axis=1)
    for i in range(nq):
        qt = qt_ref[i]

        def own_rows(c):
            pieces_c = [jnp.zeros((dq * c, tile), BF16), qt[dq * c:dq * (c + 1)],
                        jnp.zeros((LANES - dq * (c + 1), tile), BF16)]
            return jnp.concatenate([x for x in pieces_c if x.shape[0]], axis=0)

        qx_scr[i, 0:LANES, :] = jnp.concatenate([own_rows(c) for c in range(combos)], axis=1)
        qx_scr[i, LANES:2 * LANES, :] = qfeat
    pos = lax.broadcasted_iota(jnp.int32, (tile, LANES), 0).astype(F32)
    kfeat = jnp.where(lane_t < n_split, pos, 0.0).astype(BF16)
    ones = jnp.ones((ones_rows, tile), BF16)

    def score(i, t, buf):
        k_ext = jnp.concatenate([k_ref[t * tile:(t + 1) * tile, :], kfeat], axis=1)
        s_all = _dot(k_ext, qx_scr[i])
        if t == i:
            s_all = jnp.where(causal4, s_all, NEG_BIG)
        s_scr[buf] = s_all
        for c in range(combos):
            mt_scr[buf, c] = jnp.max(s_all[:, c * tile:(c + 1) * tile], axis=0, keepdims=True)

    def softmax(i, t, buf, first):
        off = float((t - i) * tile)
        ps, alphas = [], []
        for c in range(combos):
            shift = slopes[c // 2] * off
            m_tile = mt_scr[buf, c] + shift
            if first:
                m_new = m_tile
                alphas.append(None)
            else:
                m_old = m_scr[i % 2, c]
                m_new = jnp.maximum(m_old, m_tile)
                alphas.append(jnp.exp2(m_old - m_new))
            ps.append(jnp.exp2(s_scr[buf, :, c * tile:(c + 1) * tile] - (m_new - shift)).astype(BF16))
            m_scr[i % 2, c] = m_new
        return ps, alphas

    def accumulate(i, t, ps, alphas, first):
        vt = vt_ref[t]
        for hl in range(2):
            vaug = jnp.concatenate([vt[hl * hv:(hl + 1) * hv, :], ones], axis=0)
            pv = _dot(vaug, jnp.concatenate(ps[2 * hl:2 * hl + 2], axis=1))
            if first:
                acc_scr[i % 2, hl] = pv
            else:
                acc_scr[i % 2, hl] = acc_scr[i % 2, hl] * jnp.concatenate(alphas[2 * hl:2 * hl + 2], axis=1) + pv

    def finalize(i):
        outs = []
        for hl in range(2):
            a0 = acc_scr[i % 2, hl, :, 0:tile]
            a1 = acc_scr[i % 2, hl, :, tile:2 * tile]
            o = a0[0:hv] / a0[hv:hv + 1] - lam * (a1[0:hv] / a1[hv:hv + 1])
            outs.append(o * lax.rsqrt(jnp.mean(o * o, axis=0, keepdims=True) + EPS))
        ot = jnp.concatenate(outs, axis=0)
        o_ref[i * tile:(i + 1) * tile, :] = (ot.T * g_ref[...] * (1.0 - lambda_init)).astype(BF16)

    visits = [(i, t) for i in range(nq) for t in [i] + list(range(i))]
    score(*visits[0], 0)
    for n, (i, t) in enumerate(visits):
        buf = n % 2
        first = t == i
        ps, alphas = softmax(i, t, buf, first)
        if n + 1 < len(visits):
            score(*visits[n + 1], 1 - buf)
        accumulate(i, t, ps, alphas, first)
        if n + 1 == len(visits) or visits[n + 1][0] != i:
            finalize(i)


def diff_attention(qt, dk, vt, prm, l, lambda_init, nb, seq):
    n = dk.shape[0]
    tile = ATT_TILE
    assert tile == QV_TILE
    nq = seq // tile
    pairs = DIFF_HEADS // 2
    lam_names = ("lq1", "lk1", "lq2", "lk2")
    slabs = pl.BlockSpec((nq, LANES, tile), lambda b, j: (b, j, 0))
    return pl.pallas_call(
        functools.partial(_diff_kernel, tile=tile, lambda_init=lambda_init),
        out_shape=jax.ShapeDtypeStruct((n, DIFF_WIDTH), BF16),
        grid=(nb, pairs),
        in_specs=[_layer_spec(prm[k], l) for k in lam_names] + [
            slabs,
            pl.BlockSpec((seq, LANES), lambda b, j: (b, j)),
            slabs,
            pl.BlockSpec((None, 1, LANES), lambda b, j: (l, 0, j)),
        ],
        out_specs=pl.BlockSpec((seq, LANES), lambda b, j: (b, j)),
        scratch_shapes=[
            pltpu.VMEM((2, 2, DIFF_V_DIM + 16, 2 * tile), F32),
            pltpu.VMEM((2, 4, 1, tile), F32),
            pltpu.VMEM((2, tile, 4 * tile), F32),
            pltpu.VMEM((2, 4, 1, tile), F32),
            pltpu.VMEM((nq, 2 * LANES, 4 * tile), BF16),
        ],
        compiler_params=_params(("parallel", "parallel"), 40),
        name="diff_attention",
    )(*[prm[k] for k in lam_names], qt, dk, vt, prm["subln"])


def _mlp_kernel(h_ref, s_ref, r_ref, d_ref, wo_ref, g2_ref, wup_ref, cw_ref, cb_ref, wdn_ref, o_ref,
                carry_scr, *, tm, tiles_per_seq):
    @pl.when(pl.program_id(0) % tiles_per_seq == 0)
    def _():
        carry_scr[...] = jnp.zeros_like(carry_scr)

    mix = jnp.concatenate([s_ref[...], r_ref[...], d_ref[...]], axis=1)
    h1 = h_ref[...] + _dot(mix, wo_ref[...])
    hn = _rms(h1, g2_ref[...]).astype(BF16)
    row8 = lax.broadcasted_iota(jnp.int32, (SUBLANES, 2 * FF_CHUNK), 0)
    n_chunks = D_FF // FF_CHUNK

    def up(jj):
        return _dot(hn, wup_ref[:, 2 * jj * FF_CHUNK:2 * (jj + 1) * FF_CHUNK])

    def conv(u, jj):
        cols = slice(2 * jj * FF_CHUNK, 2 * (jj + 1) * FF_CHUNK)
        c6 = carry_scr[6:7, cols]
        c7 = carry_scr[7:8, cols]
        r1 = pltpu.roll(u, 1, 0)
        r2 = pltpu.roll(u, 2, 0)
        p1 = jnp.concatenate([jnp.where(row8 == 0, c7, r1[0:8]), r1[8:]], axis=0)
        p2 = jnp.concatenate([jnp.where(row8 == 0, c6, jnp.where(row8 == 1, c7, r2[0:8])), r2[8:]], axis=0)
        carry_scr[:, cols] = u[tm - 8:tm, :]
        w = cw_ref[:, cols]
        return w[0:1] * p2 + w[1:2] * p1 + w[2:3] * u + cb_ref[:, cols]

    acc = jnp.zeros((tm, D_MODEL), F32)
    u_cur = up(0)
    acts = []
    for jj in range(n_chunks):
        u_next = up(jj + 1) if jj + 1 < n_chunks else None
        c = conv(u_cur, jj)
        a = c[:, :FF_CHUNK]
        b = c[:, FF_CHUNK:]
        acts.append((a * _sigmoid(a) * b).astype(BF16))
        if len(acts) == DOWN_GROUP or jj + 1 == n_chunks:
            first = jj + 1 - len(acts)
            acc = acc + _dot(jnp.concatenate(acts, axis=1), wdn_ref[first * FF_CHUNK:(jj + 1) * FF_CHUNK, :])
            acts = []
        u_cur = u_next
    o_ref[...] = h1 + acc


def mixer_mlp(h, s_out, r_out, d_out, prm, l, seq):
    n = h.shape[0]
    tm = TM_MLP

    def rows_in(width):
        return pl.BlockSpec((tm, width), lambda i: (i, 0))

    names = ("w_out", "g2", "w_up", "conv_w", "conv_b", "w_down")
    return pl.pallas_call(
        functools.partial(_mlp_kernel, tm=tm, tiles_per_seq=seq // tm),
        out_shape=jax.ShapeDtypeStruct((n, D_MODEL), F32),
        grid=(n // tm,),
        in_specs=[rows_in(D_MODEL), rows_in(SSM_WIDTH), rows_in(RET_WIDTH), rows_in(DIFF_WIDTH)]
        + [_layer_spec(prm[k], l, single=True) for k in names],
        out_specs=rows_in(D_MODEL),
        scratch_shapes=[pltpu.VMEM((SUBLANES, 2 * D_FF), F32)],
        compiler_params=_params(("arbitrary",), 56),
        name="mixer_mlp",
    )(h, s_out, r_out, d_out, *[prm[k] for k in names])


def _ple(h_ref, p_ref, g3_ref, wpg_ref, wpe_ref):
    h2 = h_ref[...]
    gate = _sigmoid(_dot(_rms(h2, g3_ref[...]).astype(BF16), wpg_ref[...]))
    return h2 + gate * _dot(p_ref[...].astype(BF16), wpe_ref[...])


def _ple_proj_kernel(h_ref, p_ref, g3_ref, wpg_ref, wpe_ref, g1_ref, w_ref,
                     ho_ref, u_ref, ret_ref, dk_ref, qt_ref, vt_ref):
    h3 = _ple(h_ref, p_ref, g3_ref, wpg_ref, wpe_ref)
    ho_ref[...] = h3
    hn = _rms(h3, g1_ref[...]).astype(BF16)
    _project(hn, w_ref, u_ref, ret_ref, dk_ref, qt_ref, vt_ref)


def _ple_final_kernel(h_ref, p_ref, g3_ref, wpg_ref, wpe_ref, gf_ref, o_ref):
    o_ref[...] = _rms(_ple(h_ref, p_ref, g3_ref, wpg_ref, wpe_ref), gf_ref[...])


def _ple_in_specs(n, tm, l, prm):
    return [pl.BlockSpec((tm, D_MODEL), lambda i: (i, 0)), pl.BlockSpec((tm, PLE_DIM), lambda i: (l * (n // tm) + i, 0)),
            _layer_spec(prm["g3"], l), _layer_spec(prm["w_pg"], l), _layer_spec(prm["w_pe"], l)]


def ple_proj(h, p_flat, prm, l):
    n = h.shape[0]
    tm = TM_PROJ
    shapes, specs = _proj_out(n, tm)
    return pl.pallas_call(
        _ple_proj_kernel,
        out_shape=(jax.ShapeDtypeStruct((n, D_MODEL), F32),) + shapes,
        grid=(n // tm,),
        in_specs=_ple_in_specs(n, tm, l, prm) + [_layer_spec(prm["g1"], l + 1), _layer_spec(prm["w_in"], l + 1)],
        out_specs=(pl.BlockSpec((tm, D_MODEL), lambda i: (i, 0)),) + specs,
        compiler_params=_params(("parallel",), 56),
        name="ple_proj",
    )(h, p_flat, prm["g3"], prm["w_pg"], prm["w_pe"], prm["g1"], prm["w_in"])


def ple_final(h, p_flat, prm, l, final_g):
    n = h.shape[0]
    tm = TM_PROJ
    return pl.pallas_call(
        _ple_final_kernel,
        out_shape=jax.ShapeDtypeStruct((n, D_MODEL), F32),
        grid=(n // tm,),
        in_specs=_ple_in_specs(n, tm, l, prm) + [pl.BlockSpec(final_g.shape, lambda i: (0, 0))],
        out_specs=pl.BlockSpec((tm, D_MODEL), lambda i: (i, 0)),
        compiler_params=_params(("parallel",), 40),
        name="ple_final",
    )(h, p_flat, prm["g3"], prm["w_pg"], prm["w_pe"], final_g)


def kernel(x, p, norm1_g, w_in, ssm_lam_re, ssm_lam_im, ssm_log_dt, ssm_b_re, ssm_b_im, ssm_c_re, ssm_c_im, ssm_d, ssm_w_glu, ssm_b_glu, ret_gn_g, diff_lq1, diff_lk1, diff_lq2, diff_lk2, diff_subln_g, w_out, norm2_g, w_up, conv_w, conv_b, w_down, norm3_g, w_pg, w_pe, final_g):
    nb, seq, _ = x.shape
    depth = w_in.shape[0]
    n = nb * seq
    assert seq % max(TM_PROJ, RET_CHUNK, ATT_TILE, S5_CHUNK, TM_MLP) == 0 and w_in.shape[2] == PROJ_WIDTH

    def row(a):
        return a[:, None, :]

    def interleave(a):
        lead = a.shape[:-1]
        a = a.reshape(lead + (2, D_FF // FF_CHUNK, FF_CHUNK))
        return jnp.swapaxes(a, -3, -2).reshape(lead + (2 * D_FF,))

    bmat, cmat, s5_a = s5_prep(ssm_lam_re, ssm_lam_im, ssm_log_dt, ssm_b_re, ssm_b_im, ssm_c_re, ssm_c_im)
    prm = {
        "w_in": w_in.astype(BF16), "w_out": w_out.astype(BF16), "w_up": interleave(w_up.astype(BF16)),
        "w_down": w_down.astype(BF16), "w_pg": w_pg.astype(BF16), "w_pe": w_pe.astype(BF16),
        "w_glu": ssm_w_glu.astype(BF16), "s5_b": bmat.astype(BF16), "s5_c": cmat.astype(BF16), "s5_a": s5_a,
        "g1": row(norm1_g), "g2": row(norm2_g), "g3": row(norm3_g), "ssm_d": row(ssm_d), "b_glu": row(ssm_b_glu),
        "gn": row(ret_gn_g), "subln": row(diff_subln_g), "conv_w": interleave(conv_w), "conv_b": row(interleave(conv_b)),
        "lq1": row(diff_lq1), "lk1": row(diff_lk1), "lq2": row(diff_lq2), "lk2": row(diff_lk2),
    }
    p_flat = p.reshape(depth * n, PLE_DIM)

    h = x.reshape(n, D_MODEL)
    u, ret, dk, qt, vt = norm_proj(h, prm, 0)
    out = None
    for l in range(depth):
        lambda_init = 0.8 - 0.6 * math.exp(-0.3 * l)
        s_out = s5_mixer(u.reshape(nb, seq, SSM_WIDTH), prm, l)
        r_out = retention(ret, prm, l, nb, seq)
        d_out = diff_attention(qt, dk, vt, prm, l, lambda_init, nb, seq)
        h = mixer_mlp(h, s_out.reshape(n, SSM_WIDTH), r_out, d_out, prm, l, seq)
        if l + 1 < depth:
            h, u, ret, dk, qt, vt = ple_proj(h, p_flat, prm, l)
        else:
            out = ple_final(h, p_flat, prm, l, final_g[None, :])
    return out.reshape(nb, seq, D_MODEL)
```

```python
import functools
import math

import jax
import jax.numpy as jnp
import numpy as np
from jax import lax
from jax.experimental import pallas as pl
from jax.experimental.pallas import tpu as pltpu

F32 = jnp.float32
BF16 = jnp.bfloat16

D_MODEL = 1024
PLE_DIM = 256
SSM_WIDTH = 256
SSM_GROUP = 16
SSM_GROUPS = 16
SSM_STATE = 64
SSM_STATES = SSM_GROUPS * SSM_STATE
RET_HEADS = 6
RET_HEAD_DIM = 64
RET_WIDTH = RET_HEADS * RET_HEAD_DIM
DIFF_HEADS = 6
DIFF_QK_DIM = 32
DIFF_V_DIM = 64
DIFF_WIDTH = DIFF_HEADS * DIFF_V_DIM
D_FF = 2816
EPS = 1e-6
RET_LOG_GAMMA = np.log1p(-(2.0 ** (-5.0 - np.arange(RET_HEADS)))).astype(np.float32)
ALIBI_SLOPES = (2.0 ** (-8.0 * (np.arange(DIFF_HEADS) + 1) / DIFF_HEADS)).astype(np.float32)
PROJ_COLS = np.cumsum([0, SSM_WIDTH, 4 * RET_WIDTH, DIFF_WIDTH, DIFF_WIDTH, DIFF_WIDTH])
PROJ_WIDTH = int(PROJ_COLS[-1])

LANES = 128
SUBLANES = 8
VMEM_BYTES_V7X = 64 * 1024 * 1024
NEG_BIG = -1e30
LOG2E = math.log2(math.e)

TM_PROJ = 1024
TM_MLP = 256
FF_CHUNK = 256
DOWN_GROUP = 11
S5_CHUNK = 128
S5_BLOCK = 16
RET_CHUNK = 256
ATT_TILE = 256
QV_TILE = 256


def _params(semantics, vmem_mib):
    assert vmem_mib * 1024 * 1024 < VMEM_BYTES_V7X
    return pltpu.CompilerParams(dimension_semantics=semantics, vmem_limit_bytes=vmem_mib * 1024 * 1024)


def _layer_spec(arr, l, single=False):
    nd = arr.ndim
    kw = {"pipeline_mode": pl.Buffered(1)} if single else {}
    return pl.BlockSpec((None,) + arr.shape[1:], lambda *_: (l,) + (0,) * (nd - 1), **kw)


def _rms(x, g):
    return x * lax.rsqrt(jnp.mean(x * x, axis=-1, keepdims=True) + EPS) * g


def _sigmoid(x):
    return 1.0 / (1.0 + jnp.exp(-x))


def _dot(a, b):
    return jnp.dot(a, b, preferred_element_type=F32)


def _dot_nt(a, b):
    return lax.dot_general(a, b, (((1,), (1,)), ((), ())), preferred_element_type=F32)


def _project(hn, w_ref, u_ref, ret_ref, dk_ref, qt_ref, vt_ref):
    c = [int(v) for v in PROJ_COLS]
    u_ref[...] = _dot(hn, w_ref[:, c[0]:c[1]])
    ret_ref[...] = _dot(hn, w_ref[:, c[1]:c[2]]).astype(BF16)
    qk = _dot(hn, w_ref[:, c[2]:c[4]])
    dk_ref[...] = qk[:, DIFF_WIDTH:].astype(BF16)
    q = qk[:, :DIFF_WIDTH] * (DIFF_QK_DIM ** -0.5 * LOG2E)
    v = _dot(hn, w_ref[:, c[4]:c[5]])
    for r in range(qt_ref.shape[0]):
        rows = slice(r * QV_TILE, (r + 1) * QV_TILE)
        qt_ref[r] = q[rows, :].T.astype(BF16)
        vt_ref[r] = v[rows, :].T.astype(BF16)


def _norm_proj_kernel(h_ref, g_ref, w_ref, u_ref, ret_ref, dk_ref, qt_ref, vt_ref):
    hn = _rms(h_ref[...], g_ref[...]).astype(BF16)
    _project(hn, w_ref, u_ref, ret_ref, dk_ref, qt_ref, vt_ref)


def _proj_out(n, tm):
    slab = jax.ShapeDtypeStruct((n // QV_TILE, DIFF_WIDTH, QV_TILE), BF16)
    shapes = (
        jax.ShapeDtypeStruct((n, SSM_WIDTH), F32),
        jax.ShapeDtypeStruct((n, 4 * RET_WIDTH), BF16),
        jax.ShapeDtypeStruct((n, DIFF_WIDTH), BF16),
        slab,
        slab,
    )
    slab_spec = pl.BlockSpec((tm // QV_TILE, DIFF_WIDTH, QV_TILE), lambda i: (i, 0, 0))
    specs = (
        pl.BlockSpec((tm, SSM_WIDTH), lambda i: (i, 0)),
        pl.BlockSpec((tm, 4 * RET_WIDTH), lambda i: (i, 0)),
        pl.BlockSpec((tm, DIFF_WIDTH), lambda i: (i, 0)),
        slab_spec,
        slab_spec,
    )
    return shapes, specs


def norm_proj(h, prm, l):
    n = h.shape[0]
    tm = TM_PROJ
    shapes, specs = _proj_out(n, tm)
    return pl.pallas_call(
        _norm_proj_kernel,
        out_shape=shapes,
        grid=(n // tm,),
        in_specs=[pl.BlockSpec((tm, D_MODEL), lambda i: (i, 0)), _layer_spec(prm["g1"], l), _layer_spec(prm["w_in"], l)],
        out_specs=specs,
        compiler_params=_params(("parallel",), 48),
        name="norm_proj",
    )(h, prm["g1"], prm["w_in"])


def _s5_prep_kernel(lr_ref, li_ref, ldt_ref, brt_ref, bit_ref, crt_ref, cit_ref, b_ref, c_ref, a_ref):
    lr = lr_ref[0]
    li = li_ref[0]
    dt = jnp.exp(ldt_ref[0])
    mag = jnp.exp(lr * dt)
    ar = mag * jnp.cos(li * dt)
    ai = mag * jnp.sin(li * dt)
    den = lr * lr + li * li
    cr = ((ar - 1.0) * lr + ai * li) / den
    ci = (ai * lr - (ar - 1.0) * li) / den
    b_ref[...] = jnp.zeros_like(b_ref)
    c_ref[...] = jnp.zeros_like(c_ref)
    for g in range(SSM_GROUPS):
        rows = slice(g * SSM_GROUP, (g + 1) * SSM_GROUP)
        cols = slice(g * SSM_STATE, (g + 1) * SSM_STATE)
        cols_im = slice(SSM_STATES + g * SSM_STATE, SSM_STATES + (g + 1) * SSM_STATE)
        crg = cr[g:g + 1, :]
        cig = ci[g:g + 1, :]
        br = brt_ref[0, g]
        bi = bit_ref[0, g]
        b_ref[0, rows, cols] = crg * br - cig * bi
        b_ref[0, rows, cols_im] = crg * bi + cig * br
        c_ref[0, cols, rows] = crt_ref[0, g]
        c_ref[0, cols_im, rows] = -cit_ref[0, g]
        a_ref[0, 0:1, cols] = ar[g:g + 1, :]
        a_ref[0, 1:2, cols] = ai[g:g + 1, :]


def s5_prep(lam_re, lam_im, log_dt, b_re, b_im, c_re, c_im):
    depth = lam_re.shape[0]
    brt = jnp.transpose(b_re, (0, 1, 3, 2))
    bit = jnp.transpose(b_im, (0, 1, 3, 2))
    crt = jnp.transpose(c_re, (0, 1, 3, 2))
    cit = jnp.transpose(c_im, (0, 1, 3, 2))
    ldt = log_dt[..., None]

    def spec(a):
        nd = a.ndim
        return pl.BlockSpec((1,) + a.shape[1:], lambda l: (l,) + (0,) * (nd - 1))

    ins = (lam_re, lam_im, ldt, brt, bit, crt, cit)
    out_shape = (
        jax.ShapeDtypeStruct((depth, SSM_WIDTH, 2 * SSM_STATES), F32),
        jax.ShapeDtypeStruct((depth, 2 * SSM_STATES, SSM_WIDTH), F32),
        jax.ShapeDtypeStruct((depth, 2, SSM_STATES), F32),
    )
    return pl.pallas_call(
        _s5_prep_kernel,
        out_shape=out_shape,
        grid=(depth,),
        in_specs=[spec(a) for a in ins],
        out_specs=tuple(spec(o) for o in out_shape),
        compiler_params=_params(("parallel",), 32),
        name="s5_prep",
    )(*ins)


def _gelu_tanh(x):
    return 0.5 * x * (1.0 + jnp.tanh(math.sqrt(2.0 / math.pi) * (x + 0.044715 * (x * x * x))))


def _s5_kernel(un_ref, up_ref, bmat_ref, cmat_ref, a_ref, d_ref, wglu_ref, bglu_ref, o_ref,
               buf_scr, st_scr, *, tc):
    nb = un_ref.shape[0]
    g = pl.program_id(0)
    n_blk = tc // S5_BLOCK
    col_blk = 2 * SSM_STATES // n_blk

    @pl.when(g == 0)
    def _():
        buf_scr[...] = jnp.zeros_like(buf_scr)

    @pl.when(g <= 1)
    def _():
        st_scr[...] = jnp.zeros_like(st_scr)

    def stages(b_in, b_scan, b_out):
        ar = jnp.broadcast_to(a_ref[0:1, :], (nb, SSM_STATES))
        ai = jnp.broadcast_to(a_ref[1:2, :], (nb, SSM_STATES))
        xr = st_scr[:, 0:SSM_STATES]
        xi = st_scr[:, SSM_STATES:2 * SSM_STATES]
        u_next = jnp.swapaxes(un_ref[...], 0, 1).reshape(tc * nb, SSM_WIDTH).astype(BF16)
        y = d_ref[...] * jnp.swapaxes(up_ref[...], 0, 1).reshape(tc * nb, SSM_WIDTH)
        for j in range(n_blk):
            cols = slice(j * col_blk, (j + 1) * col_blk)
            y = y + _dot(buf_scr[b_out, :, cols].astype(BF16), cmat_ref[cols, :])
            for t in range(j * S5_BLOCK, (j + 1) * S5_BLOCK):
                r = t * nb
                nxr = ar * xr - ai * xi + buf_scr[b_scan, r:r + nb, 0:SSM_STATES]
                nxi = ar * xi + ai * xr + buf_scr[b_scan, r:r + nb, SSM_STATES:2 * SSM_STATES]
                buf_scr[b_scan, r:r + nb, 0:SSM_STATES] = nxr
                buf_scr[b_scan, r:r + nb, SSM_STATES:2 * SSM_STATES] = nxi
                xr, xi = nxr, nxi
            buf_scr[b_in, :, cols] = _dot(u_next, bmat_ref[:, cols])
        st_scr[:, 0:SSM_STATES] = xr
        st_scr[:, SSM_STATES:2 * SSM_STATES] = xi
        z = _gelu_tanh(y)
        out = z * _sigmoid(_dot(z.astype(BF16), wglu_ref[...]) + bglu_ref[...])
        o_ref[...] = jnp.swapaxes(out.reshape(tc, nb, SSM_WIDTH), 0, 1).astype(BF16)

    for r in range(3):
        @pl.when(g % 3 == r)
        def _(r=r):
            stages(r, (r + 2) % 3, (r + 1) % 3)


def s5_mixer(u, prm, l):
    nb, seq, _ = u.shape
    tc = S5_CHUNK
    nc = seq // tc
    assert nb == SUBLANES

    def chunk(fn):
        return pl.BlockSpec((nb, tc, SSM_WIDTH), lambda g: (0, fn(g), 0))

    names = ("s5_b", "s5_c", "s5_a", "ssm_d", "w_glu", "b_glu")
    return pl.pallas_call(
        functools.partial(_s5_kernel, tc=tc),
        out_shape=jax.ShapeDtypeStruct(u.shape, BF16),
        grid=(nc + 2,),
        in_specs=[chunk(lambda g: jnp.minimum(g, nc - 1)), chunk(lambda g: jnp.clip(g - 2, 0, nc - 1))]
        + [_layer_spec(prm[k], l) for k in names],
        out_specs=chunk(lambda g: jnp.clip(g - 2, 0, nc - 1)),
        scratch_shapes=[
            pltpu.VMEM((3, tc * nb, 2 * SSM_STATES), F32),
            pltpu.VMEM((nb, 2 * SSM_STATES), F32),
        ],
        compiler_params=_params(("arbitrary",), 48),
        name="s5_mixer",
    )(u, u, *[prm[k] for k in names])


def _ret_kernel(q_ref, k_ref, v_ref, g_ref, gn_ref, o_ref, dmat_scr, decq_scr, deck_scr, sdec_scr, st_scr, *, cr):
    pairs = RET_HEADS // 2
    hd = RET_HEAD_DIM
    scale = hd ** -0.5
    lane = lax.broadcasted_iota(jnp.int32, (1, LANES), 1)
    lo = lane < hd
    rows = lax.broadcasted_iota(jnp.int32, (LANES, LANES), 0) < hd
    cols = lax.broadcasted_iota(jnp.int32, (LANES, LANES), 1) < hd
    same = rows == cols

    @pl.when(pl.program_id(0) == 0)
    def _():
        t = lax.broadcasted_iota(jnp.int32, (cr, cr), 0)
        s = lax.broadcasted_iota(jnp.int32, (cr, cr), 1)
        dist = (t - s).astype(F32)
        pos = lax.broadcasted_iota(jnp.int32, (cr, LANES), 0).astype(F32)
        for h in range(RET_HEADS):
            lg = float(RET_LOG_GAMMA[h])
            dmat_scr[h // 2, :, (h % 2) * cr:(h % 2 + 1) * cr] = (
                jnp.where(dist >= 0, jnp.exp(jnp.maximum(dist, 0.0) * lg), 0.0) * scale)
        for j in range(pairs):
            lga, lgb = float(RET_LOG_GAMMA[2 * j]), float(RET_LOG_GAMMA[2 * j + 1])
            lg = jnp.where(lo, lga, lgb)
            decq_scr[j] = jnp.exp((pos + 1.0) * lg)
            deck_scr[j] = jnp.exp((cr - 1.0 - pos) * lg) * scale
            sdec_scr[j] = jnp.where(same, jnp.where(rows, math.exp(cr * lga), math.exp(cr * lgb)), 0.0)

    st_scr[...] = jnp.zeros_like(st_scr)

    avg2 =jnp.where(jnp.concatenate([same, same], axis=0), 1.0 / hd, 0.0).astype(BF16)
    blockmask = jnp.where(same, 1.0, 0.0).astype(F32)

    def group_mean(x):
        hi = x.astype(BF16)
        lo_part = (x - hi.astype(F32)).astype(BF16)
        return _dot(jnp.concatenate([hi, lo_part], axis=1), avg2)

    sls = [slice(j * LANES, (j + 1) * LANES) for j in range(pairs)]

    def chunk(c, carry):
        rows = pl.ds(pl.multiple_of(c * cr, cr), cr)
        scores = []
        for j in range(pairs):
            k = k_ref[rows, sls[j]]
            zero = jnp.zeros_like(k)
            k2 = jnp.concatenate([jnp.where(lo, k, zero), jnp.where(lo, zero, k)], axis=0)
            scores.append(_dot_nt(q_ref[rows, sls[j]], k2))
        probs = [(scores[j] * dmat_scr[j]).astype(BF16) for j in range(pairs)]
        outs = []
        for j in range(pairs):
            q = q_ref[rows, sls[j]]
            k = k_ref[rows, sls[j]]
            v = v_ref[rows, sls[j]]
            zero = jnp.zeros_like(v)
            v2 = jnp.concatenate([jnp.where(lo, v, zero), jnp.where(lo, zero, v)], axis=0)
            st = st_scr[j]
            cross = _dot((q.astype(F32) * decq_scr[j]).astype(BF16), st.astype(BF16))
            kdt = (k.astype(F32) * deck_scr[j]).T.astype(BF16)
            st_scr[j] = st * sdec_scr[j] + _dot(kdt, v) * blockmask
            outs.append(_dot(probs[j], v2) + cross)
        devs = [o - group_mean(o) for o in outs]
        variances = [group_mean(d * d) for d in devs]
        for j in range(pairs):
            on = devs[j] * lax.rsqrt(variances[j] + EPS) * gn_ref[:, sls[j]]
            gate = g_ref[rows, sls[j]].astype(F32)
            o_ref[rows, sls[j]] = (gate * _sigmoid(gate) * on).astype(BF16)
        return carry

    lax.fori_loop(0, q_ref.shape[0] // cr, chunk, 0)


def retention(ret, prm, l, nb, seq):
    n = ret.shape[0]
    cr = RET_CHUNK
    pairs = RET_HEADS // 2

    def part(col):
        return pl.BlockSpec((seq, RET_WIDTH), lambda b: (b, col))

    return pl.pallas_call(
        functools.partial(_ret_kernel, cr=cr),
        out_shape=jax.ShapeDtypeStruct((n, RET_WIDTH), BF16),
        grid=(nb,),
        in_specs=[part(0), part(1), part(2), part(3), _layer_spec(prm["gn"], l)],
        out_specs=pl.BlockSpec((seq, RET_WIDTH), lambda b: (b, 0)),
        scratch_shapes=[
            pltpu.VMEM((pairs, cr, 2 * cr), F32),
            pltpu.VMEM((pairs, cr, LANES), F32),
            pltpu.VMEM((pairs, cr, LANES), F32),
            pltpu.VMEM((pairs, LANES, LANES), F32),
            pltpu.VMEM((pairs, LANES, LANES), F32),
        ],
        compiler_params=_params(("arbitrary",), 40),
        name="retention",
    )(ret, ret, ret, ret, prm["gn"])


def _split_bf16(x, parts=3):
    out = []
    rem = np.float32(x)
    for _ in range(parts):
        hi = np.float32(rem.astype(BF16))
        out.append(float(hi))
        rem = np.float32(rem - hi)
    return out


def _diff_kernel(lq1_ref, lk1_ref, lq2_ref, lk2_ref, qt_ref, k_ref, vt_ref, g_ref, o_ref,
                 acc_scr, m_scr, s_scr, mt_scr, qx_scr, *, tile, lambda_init):
    j = pl.program_id(1)
    nq = qt_ref.shape[0]
    hv = DIFF_V_DIM
    dq = DIFF_QK_DIM
    ones_rows = 16
    combos = 4
    n_split = 3

    lam = (jnp.exp(jnp.sum(lq1_ref[...] * lk1_ref[...], axis=-1, keepdims=True))
           - jnp.exp(jnp.sum(lq2_ref[...] * lk2_ref[...], axis=-1, keepdims=True)) + lambda_init)

    def pick(vals):
        return jnp.where(j == 0, vals[0], jnp.where(j == 1, vals[1], vals[2]))

    slope2 = [float(np.float32(ALIBI_SLOPES[h]) * np.float32(LOG2E)) for h in range(DIFF_HEADS)]
    parts = [_split_bf16(s, n_split) for s in slope2]
    slopes = [pick([slope2[2 * jj + hl] for jj in range(3)]) for hl in range(2)]
    pieces = [[pick([parts[2 * jj + hl][n] for jj in range(3)]) for n in range(n_split)] for hl in range(2)]

    s_loc = lax.broadcasted_iota(jnp.int32, (tile, combos * tile), 0)
    t_loc = lax.broadcasted_iota(jnp.int32, (tile, combos * tile), 1) & (tile - 1)
    causal4 = s_loc <= t_loc

    row_t = lax.broadcasted_iota(jnp.int32, (LANES, tile), 0)
    lane_t = lax.broadcasted_iota(jnp.int32, (tile, LANES), 1)
    qfeat = []
    for hl in range(2):
        f = jnp.zeros((LANES, tile), F32)
        for n in range(n_split):
            f = jnp.where(row_t == n, pieces[hl][n], f)
        qfeat += [f.astype(BF16)] * 2
    qfeat = jnp.concatenate(qfeat, axis=1)
    for i in range(nq):
        qt = qt_ref[i]

        def own_rows(c):
            pieces_c = [jnp.zeros((dq * c, tile), BF16), qt[dq * c:dq * (c + 1)],
                        jnp.zeros((LANES - dq * (c + 1), tile), BF16)]
            return jnp.concatenate([x for x in pieces_c if x.shape[0]], axis=0)

        qx_scr[i, 0:LANES, :] = jnp.concatenate([own_rows(c) for c in range(combos)], axis=1)
        qx_scr[i, LANES:2 * LANES, :] = qfeat
    pos = lax.broadcasted_iota(jnp.int32, (tile, LANES), 0).astype(F32)
    kfeat = jnp.where(lane_t < n_split, pos, 0.0).astype(BF16)
    ones = jnp.ones((ones_rows, tile), BF16)

    def score(i, t, buf):
        k_ext = jnp.concatenate([k_ref[t * tile:(t + 1) * tile, :], kfeat], axis=1)
        s_all = _dot(k_ext, qx_scr[i])
        if t == i:
            s_all = jnp.where(causal4, s_all, NEG_BIG)
        s_scr[buf] = s_all
        for c in range(combos):
            mt_scr[buf, c] = jnp.max(s_all[:, c * tile:(c + 1) * tile], axis=0, keepdims=True)

    def softmax(i, t, buf, first):
        off = float((t - i) * tile)
        ps, alphas = [], []
        for c in range(combos):
            shift = slopes[c // 2] * off
            m_tile = mt_scr[buf, c] + shift
            if first:
                m_new = m_tile
                alphas.append(None)
            else:
                m_old = m_scr[i % 2, c]
                m_new = jnp.maximum(m_old, m_tile)
                alphas.append(jnp.exp2(m_old - m_new))
            ps.append(jnp.exp2(s_scr[buf, :, c * tile:(c + 1) * tile] - (m_new - shift)).astype(BF16))
            m_scr[i % 2, c] = m_new
        return ps, alphas

    def accumulate(i, t, ps, alphas, first):
        vt = vt_ref[t]
        for hl in range(2):
            vaug = jnp.concatenate([vt[hl * hv:(hl + 1) * hv, :], ones], axis=0)
            pv = _dot(vaug, jnp.concatenate(ps[2 * hl:2 * hl + 2], axis=1))
            if first:
                acc_scr[i % 2, hl] = pv
            else:
                acc_scr[i % 2, hl] = acc_scr[i % 2, hl] * jnp.concatenate(alphas[2 * hl:2 * hl + 2], axis=1) + pv

    def finalize(i):
        outs = []
        for hl in range(2):
            a0 = acc_scr[i % 2, hl, :, 0:tile]
            a1 = acc_scr[i % 2, hl, :, tile:2 * tile]
            o = a0[0:hv] / a0[hv:hv + 1] - lam * (a1[0:hv] / a1[hv:hv + 1])
            outs.append(o * lax.rsqrt(jnp.mean(o * o, axis=0, keepdims=True) + EPS))
        ot = jnp.concatenate(outs, axis=0)
        o_ref[i * tile:(i + 1) * tile, :] = (ot.T * g_ref[...] * (1.0 - lambda_init)).astype(BF16)

    visits = [(i, t) for i in range(nq) for t in [i] + list(range(i))]
    score(*visits[0], 0)
    for n, (i, t) in enumerate(visits):
        buf = n % 2
        first = t == i
        if n + 1 < len(visits):
            score(*visits[n + 1], 1 - buf)
        ps, alphas = softmax(i, t, buf, first)
        accumulate(i, t, ps, alphas, first)
        if n + 1 == len(visits) or visits[n + 1][0] != i:
            finalize(i)


def diff_attention(qt, dk, vt, prm, l, lambda_init, nb, seq):
    n = dk.shape[0]
    tile = ATT_TILE
    assert tile == QV_TILE
    nq = seq // tile
    pairs = DIFF_HEADS // 2
    lam_names = ("lq1", "lk1", "lq2", "lk2")
    slabs = pl.BlockSpec((nq, LANES, tile), lambda b, j: (b, j, 0))
    return pl.pallas_call(
        functools.partial(_diff_kernel, tile=tile, lambda_init=lambda_init),
        out_shape=jax.ShapeDtypeStruct((n, DIFF_WIDTH), BF16),
        grid=(nb, pairs),
        in_specs=[_layer_spec(prm[k], l) for k in lam_names] + [
            slabs,
            pl.BlockSpec((seq, LANES), lambda b, j: (b, j)),
            slabs,
            pl.BlockSpec((None, 1, LANES), lambda b, j: (l, 0, j)),
        ],
        out_specs=pl.BlockSpec((seq, LANES), lambda b, j: (b, j)),
        scratch_shapes=[
            pltpu.VMEM((2, 2, DIFF_V_DIM + 16, 2 * tile), F32),
            pltpu.VMEM((2, 4, 1, tile), F32),
            pltpu.VMEM((2, tile, 4 * tile), F32),
            pltpu.VMEM((2, 4, 1, tile), F32),
            pltpu.VMEM((nq, 2 * LANES, 4 * tile), BF16),
        ],
        compiler_params=_params(("parallel", "parallel"), 40),
        name="diff_attention",
    )(*[prm[k] for k in lam_names], qt, dk, vt, prm["subln"])


def _mlp_kernel(h_ref, s_ref, r_ref, d_ref, wo_ref, g2_ref, wup_ref, cw_ref, cb_ref, wdn_ref, o_ref,
                carry_scr, *, tm, tiles_per_seq):
    @pl.when(pl.program_id(0) % tiles_per_seq == 0)
    def _():
        carry_scr[...] = jnp.zeros_like(carry_scr)

    mix = jnp.concatenate([s_ref[...], r_ref[...], d_ref[...]], axis=1)
    h1 = h_ref[...] + _dot(mix, wo_ref[...])
    hn = _rms(h1, g2_ref[...]).astype(BF16)
    row8 = lax.broadcasted_iota(jnp.int32, (SUBLANES, FF_CHUNK), 0)
    n_chunks = D_FF // FF_CHUNK

    def up(jj):
        return [_dot(hn, wup_ref[:, c0:c0 + FF_CHUNK]) for c0 in (jj * FF_CHUNK, D_FF + jj * FF_CHUNK)]

    def conv(u, c0):
        cols = slice(c0, c0 + FF_CHUNK)
        c6 = carry_scr[6:7, cols]
        c7 = carry_scr[7:8, cols]
        r1 = pltpu.roll(u, 1, 0)
        r2 = pltpu.roll(u, 2, 0)
        p1 = jnp.concatenate([jnp.where(row8 == 0, c7, r1[0:8]), r1[8:]], axis=0)
        p2 = jnp.concatenate([jnp.where(row8 == 0, c6, jnp.where(row8 == 1, c7, r2[0:8])), r2[8:]], axis=0)
        carry_scr[:, cols] = u[tm - 8:tm, :]
        w = cw_ref[:, cols]
        return w[0:1] * p2 + w[1:2] * p1 + w[2:3] * u + cb_ref[:, cols]

    acc = jnp.zeros((tm, D_MODEL), F32)
    u_cur = up(0)
    acts = []
    for jj in range(n_chunks):
        u_next = up(jj + 1) if jj + 1 < n_chunks else None
        a = conv(u_cur[0], jj * FF_CHUNK)
        b = conv(u_cur[1], D_FF + jj * FF_CHUNK)
        acts.append((a * _sigmoid(a) * b).astype(BF16))
        if len(acts) == DOWN_GROUP or jj + 1 == n_chunks:
            first = jj + 1 - len(acts)
            acc = acc + _dot(jnp.concatenate(acts, axis=1), wdn_ref[first * FF_CHUNK:(jj + 1) * FF_CHUNK, :])
            acts = []
        u_cur = u_next
    o_ref[...] = h1 + acc


def mixer_mlp(h, s_out, r_out, d_out, prm, l, seq):
    n = h.shape[0]
    tm = TM_MLP

    def rows_in(width):
        return pl.BlockSpec((tm, width), lambda i: (i, 0))

    names = ("w_out", "g2", "w_up", "conv_w", "conv_b", "w_down")
    return pl.pallas_call(
        functools.partial(_mlp_kernel, tm=tm, tiles_per_seq=seq // tm),
        out_shape=jax.ShapeDtypeStruct((n, D_MODEL), F32),
        grid=(n // tm,),
        in_specs=[rows_in(D_MODEL), rows_in(SSM_WIDTH), rows_in(RET_WIDTH), rows_in(DIFF_WIDTH)]
        + [_layer_spec(prm[k], l, single=True) for k in names],
        out_specs=rows_in(D_MODEL),
        scratch_shapes=[pltpu.VMEM((SUBLANES, 2 * D_FF), F32)],
        compiler_params=_params(("arbitrary",), 56),
        name="mixer_mlp",
    )(h, s_out, r_out, d_out, *[prm[k] for k in names])


def _ple(h_ref, p_ref, g3_ref, wpg_ref, wpe_ref):
    h2 = h_ref[...]
    gate = _sigmoid(_dot(_rms(h2, g3_ref[...]).astype(BF16), wpg_ref[...]))
    return h2 + gate * _dot(p_ref[...].astype(BF16), wpe_ref[...])


def _ple_proj_kernel(h_ref, p_ref, g3_ref, wpg_ref, wpe_ref, g1_ref, w_ref,
                     ho_ref, u_ref, ret_ref, dk_ref, qt_ref, vt_ref):
    h3 = _ple(h_ref, p_ref, g3_ref, wpg_ref, wpe_ref)
    ho_ref[...] = h3
    hn = _rms(h3, g1_ref[...]).astype(BF16)
    _project(hn, w_ref, u_ref, ret_ref, dk_ref, qt_ref, vt_ref)


def _ple_final_kernel(h_ref, p_ref, g3_ref, wpg_ref, wpe_ref, gf_ref, o_ref):
    o_ref[...] = _rms(_ple(h_ref, p_ref, g3_ref, wpg_ref, wpe_ref), gf_ref[...])


def _ple_in_specs(n, tm, l, prm):
    return [pl.BlockSpec((tm, D_MODEL), lambda i: (i, 0)), pl.BlockSpec((tm, PLE_DIM), lambda i: (l * (n // tm) + i, 0)),
            _layer_spec(prm["g3"], l), _layer_spec(prm["w_pg"], l), _layer_spec(prm["w_pe"], l)]


def ple_proj(h, p_flat, prm, l):
    n = h.shape[0]
    tm = TM_PROJ
    shapes, specs = _proj_out(n, tm)
    return pl.pallas_call(
        _ple_proj_kernel,
        out_shape=(jax.ShapeDtypeStruct((n, D_MODEL), F32),) + shapes,
        grid=(n // tm,),
        in_specs=_ple_in_specs(n, tm, l, prm) + [_layer_spec(prm["g1"], l + 1), _layer_spec(prm["w_in"], l + 1)],
        out_specs=(pl.BlockSpec((tm, D_MODEL), lambda i: (i, 0)),) + specs,
        compiler_params=_params(("parallel",), 56),
        name="ple_proj",
    )(h, p_flat, prm["g3"], prm["w_pg"], prm["w_pe"], prm["g1"], prm["w_in"])


def ple_final(h, p_flat, prm, l, final_g):
    n = h.shape[0]
    tm = TM_PROJ
    return pl.pallas_call(
        _ple_final_kernel,
        out_shape=jax.ShapeDtypeStruct((n, D_MODEL), F32),
        grid=(n // tm,),
        in_specs=_ple_in_specs(n, tm, l, prm) + [pl.BlockSpec(final_g.shape, lambda i: (0, 0))],
        out_specs=pl.BlockSpec((tm, D_MODEL), lambda i: (i, 0)),
        compiler_params=_params(("parallel",), 40),
        name="ple_final",
    )(h, p_flat, prm["g3"], prm["w_pg"], prm["w_pe"], final_g)


def kernel(x, p, norm1_g, w_in, ssm_lam_re, ssm_lam_im, ssm_log_dt, ssm_b_re, ssm_b_im, ssm_c_re, ssm_c_im, ssm_d, ssm_w_glu, ssm_b_glu, ret_gn_g, diff_lq1, diff_lk1, diff_lq2, diff_lk2, diff_subln_g, w_out, norm2_g, w_up, conv_w, conv_b, w_down, norm3_g, w_pg, w_pe, final_g):
    nb, seq, _ = x.shape
    depth = w_in.shape[0]
    n = nb * seq
    assert seq % max(TM_PROJ, RET_CHUNK, ATT_TILE, S5_CHUNK, TM_MLP) == 0 and w_in.shape[2] == PROJ_WIDTH

    def row(a):
        return a[:, None, :]

    bmat, cmat, s5_a = s5_prep(ssm_lam_re, ssm_lam_im, ssm_log_dt, ssm_b_re, ssm_b_im, ssm_c_re, ssm_c_im)
    prm = {
        "w_in": w_in.astype(BF16), "w_out": w_out.astype(BF16), "w_up": w_up.astype(BF16),
        "w_down": w_down.astype(BF16), "w_pg": w_pg.astype(BF16), "w_pe": w_pe.astype(BF16),
        "w_glu": ssm_w_glu.astype(BF16), "s5_b": bmat.astype(BF16), "s5_c": cmat.astype(BF16), "s5_a": s5_a,
        "g1": row(norm1_g), "g2": row(norm2_g), "g3": row(norm3_g), "ssm_d": row(ssm_d), "b_glu": row(ssm_b_glu),
        "gn": row(ret_gn_g), "subln": row(diff_subln_g), "conv_w": conv_w, "conv_b": row(conv_b),
        "lq1": row(diff_lq1), "lk1": row(diff_lk1), "lq2": row(diff_lq2), "lk2": row(diff_lk2),
    }
    p_flat = p.reshape(depth * n, PLE_DIM)

    h = x.reshape(n, D_MODEL)
    u, ret, dk, qt, vt = norm_proj(h, prm, 0)
    out = None
    for l in range(depth):
        lambda_init = 0.8 - 0.6 * math.exp(-0.3 * l)
        s_out = s5_mixer(u.reshape(nb, seq, SSM_WIDTH), prm, l)
        r_out = retention(ret, prm, l, nb, seq)
        d_out = diff_attention(qt, dk, vt, prm, l, lambda_init, nb, seq)
        h = mixer_mlp(h, s_out.reshape(n, SSM_WIDTH), r_out, d_out, prm, l, seq)
        if l + 1 < depth:
            h, u, ret, dk, qt, vt = ple_proj(h, p_flat, prm, l)
        else:
            out = ple_final(h, p_flat, prm, l, final_g[None, :])
    return out.reshape(nb, seq, D_MODEL)
```

```python
import functools
import math

import jax
import jax.numpy as jnp
import numpy as np
from jax import lax
from jax.experimental import pallas as pl
from jax.experimental.pallas import tpu as pltpu

F32 = jnp.float32
BF16 = jnp.bfloat16

D_MODEL = 1024
PLE_DIM = 256
SSM_WIDTH = 256
SSM_GROUP = 16
SSM_GROUPS = 16
SSM_STATE = 64
SSM_STATES = SSM_GROUPS * SSM_STATE
RET_HEADS = 6
RET_HEAD_DIM = 64
RET_WIDTH = RET_HEADS * RET_HEAD_DIM
DIFF_HEADS = 6
DIFF_QK_DIM = 32
DIFF_V_DIM = 64
DIFF_WIDTH = DIFF_HEADS * DIFF_V_DIM
D_FF = 2816
EPS = 1e-6
RET_LOG_GAMMA = np.log1p(-(2.0 ** (-5.0 - np.arange(RET_HEADS)))).astype(np.float32)
ALIBI_SLOPES = (2.0 ** (-8.0 * (np.arange(DIFF_HEADS) + 1) / DIFF_HEADS)).astype(np.float32)
PROJ_COLS = np.cumsum([0, SSM_WIDTH, 4 * RET_WIDTH, DIFF_WIDTH, DIFF_WIDTH, DIFF_WIDTH])
PROJ_WIDTH = int(PROJ_COLS[-1])

LANES = 128
SUBLANES = 8
VMEM_BYTES_V7X = 64 * 1024 * 1024
NEG_BIG = -1e30
LOG2E = math.log2(math.e)

TM_PROJ = 1024
TM_MLP = 256
FF_CHUNK = 256
DOWN_GROUP = 11
S5_CHUNK = 128
S5_BLOCK = 16
RET_CHUNK = 256
ATT_TILE = 256
QV_TILE = 256


def _params(semantics, vmem_mib):
    assert vmem_mib * 1024 * 1024 < VMEM_BYTES_V7X
    return pltpu.CompilerParams(dimension_semantics=semantics, vmem_limit_bytes=vmem_mib * 1024 * 1024)


def _layer_spec(arr, l, single=False):
    nd = arr.ndim
    kw = {"pipeline_mode": pl.Buffered(1)} if single else {}
    return pl.BlockSpec((None,) + arr.shape[1:], lambda *_: (l,) + (0,) * (nd - 1), **kw)


def _rms(x, g):
    return x * lax.rsqrt(jnp.mean(x * x, axis=-1, keepdims=True) + EPS) * g


def _sigmoid(x):
    return 1.0 / (1.0 + jnp.exp(-x))


def _dot(a, b):
    return jnp.dot(a, b, preferred_element_type=F32)


def _dot_nt(a, b):
    return lax.dot_general(a, b, (((1,), (1,)), ((), ())), preferred_element_type=F32)


def _project(hn, w_ref, u_ref, ret_ref, dk_ref, qt_ref, vt_ref):
    c = [int(v) for v in PROJ_COLS]
    u_ref[...] = _dot(hn, w_ref[:, c[0]:c[1]])
    ret_ref[...] = _dot(hn, w_ref[:, c[1]:c[2]]).astype(BF16)
    qk = _dot(hn, w_ref[:, c[2]:c[4]])
    dk_ref[...] = qk[:, DIFF_WIDTH:].astype(BF16)
    q = qk[:, :DIFF_WIDTH] * (DIFF_QK_DIM ** -0.5 * LOG2E)
    v = _dot(hn, w_ref[:, c[4]:c[5]])
    for r in range(qt_ref.shape[0]):
        rows = slice(r * QV_TILE, (r + 1) * QV_TILE)
        qt_ref[r] = q[rows, :].T.astype(BF16)
        vt_ref[r] = v[rows, :].T.astype(BF16)


def _norm_proj_kernel(h_ref, g_ref, w_ref, u_ref, ret_ref, dk_ref, qt_ref, vt_ref):
    hn = _rms(h_ref[...], g_ref[...]).astype(BF16)
    _project(hn, w_ref, u_ref, ret_ref, dk_ref, qt_ref, vt_ref)


def _proj_out(n, tm):
    slab = jax.ShapeDtypeStruct((n // QV_TILE, DIFF_WIDTH, QV_TILE), BF16)
    shapes = (
        jax.ShapeDtypeStruct((n, SSM_WIDTH), F32),
        jax.ShapeDtypeStruct((n, 4 * RET_WIDTH), BF16),
        jax.ShapeDtypeStruct((n, DIFF_WIDTH), BF16),
        slab,
        slab,
    )
    slab_spec = pl.BlockSpec((tm // QV_TILE, DIFF_WIDTH, QV_TILE), lambda i: (i, 0, 0))
    specs = (
        pl.BlockSpec((tm, SSM_WIDTH), lambda i: (i, 0)),
        pl.BlockSpec((tm, 4 * RET_WIDTH), lambda i: (i, 0)),
        pl.BlockSpec((tm, DIFF_WIDTH), lambda i: (i, 0)),
        slab_spec,
        slab_spec,
    )
    return shapes, specs


def norm_proj(h, prm, l):
    n = h.shape[0]
    tm = TM_PROJ
    shapes, specs = _proj_out(n, tm)
    return pl.pallas_call(
        _norm_proj_kernel,
        out_shape=shapes,
        grid=(n // tm,),
        in_specs=[pl.BlockSpec((tm, D_MODEL), lambda i: (i, 0)), _layer_spec(prm["g1"], l), _layer_spec(prm["w_in"], l)],
        out_specs=specs,
        compiler_params=_params(("parallel",), 48),
        name="norm_proj",
    )(h, prm["g1"], prm["w_in"])


def _s5_prep_kernel(lr_ref, li_ref, ldt_ref, brt_ref, bit_ref, crt_ref, cit_ref, b_ref, c_ref, a_ref):
    lr = lr_ref[0]
    li = li_ref[0]
    dt = jnp.exp(ldt_ref[0])
    mag = jnp.exp(lr * dt)
    ar = mag * jnp.cos(li * dt)
    ai = mag * jnp.sin(li * dt)
    den = lr * lr + li * li
    cr = ((ar - 1.0) * lr + ai * li) / den
    ci = (ai * lr - (ar - 1.0) * li) / den
    b_ref[...] = jnp.zeros_like(b_ref)
    c_ref[...] = jnp.zeros_like(c_ref)
    for g in range(SSM_GROUPS):
        rows = slice(g * SSM_GROUP, (g + 1) * SSM_GROUP)
        cols = slice(g * SSM_STATE, (g + 1) * SSM_STATE)
        cols_im = slice(SSM_STATES + g * SSM_STATE, SSM_STATES + (g + 1) * SSM_STATE)
        crg = cr[g:g + 1, :]
        cig = ci[g:g + 1, :]
        br = brt_ref[0, g]
        bi = bit_ref[0, g]
        b_ref[0, rows, cols] = crg * br - cig * bi
        b_ref[0, rows, cols_im] = crg * bi + cig * br
        c_ref[0, cols, rows] = crt_ref[0, g]
        c_ref[0, cols_im, rows] = -cit_ref[0, g]
        a_ref[0, 0:1, cols] = ar[g:g + 1, :]
        a_ref[0, 1:2, cols] = ai[g:g + 1, :]


def s5_prep(lam_re, lam_im, log_dt, b_re, b_im, c_re, c_im):
    depth = lam_re.shape[0]
    brt = jnp.transpose(b_re, (0, 1, 3, 2))
    bit = jnp.transpose(b_im, (0, 1, 3, 2))
    crt = jnp.transpose(c_re, (0, 1, 3, 2))
    cit = jnp.transpose(c_im, (0, 1, 3, 2))
    ldt = log_dt[..., None]

    def spec(a):
        nd = a.ndim
        return pl.BlockSpec((1,) + a.shape[1:], lambda l: (l,) + (0,) * (nd - 1))

    ins = (lam_re, lam_im, ldt, brt, bit, crt, cit)
    out_shape = (
        jax.ShapeDtypeStruct((depth, SSM_WIDTH, 2 * SSM_STATES), F32),
        jax.ShapeDtypeStruct((depth, 2 * SSM_STATES, SSM_WIDTH), F32),
        jax.ShapeDtypeStruct((depth, 2, SSM_STATES), F32),
    )
    return pl.pallas_call(
        _s5_prep_kernel,
        out_shape=out_shape,
        grid=(depth,),
        in_specs=[spec(a) for a in ins],
        out_specs=tuple(spec(o) for o in out_shape),
        compiler_params=_params(("parallel",), 32),
        name="s5_prep",
    )(*ins)


def _gelu_tanh(x):
    return 0.5 * x * (1.0 + jnp.tanh(math.sqrt(2.0 / math.pi) * (x + 0.044715 * (x * x * x))))


def _s5_kernel(un_ref, up_ref, bmat_ref, cmat_ref, a_ref, d_ref, wglu_ref, bglu_ref, o_ref,
               buf_scr, st_scr, *, tc):
    nb = un_ref.shape[0]
    g = pl.program_id(0)
    n_blk = tc // S5_BLOCK
    col_blk = 2 * SSM_STATES // n_blk

    @pl.when(g == 0)
    def _():
        buf_scr[...] = jnp.zeros_like(buf_scr)

    @pl.when(g <= 1)
    def _():
        st_scr[...] = jnp.zeros_like(st_scr)

    def stages(b_in, b_scan, b_out):
        ar = jnp.broadcast_to(a_ref[0:1, :], (nb, SSM_STATES))
        ai = jnp.broadcast_to(a_ref[1:2, :], (nb, SSM_STATES))
        xr = st_scr[:, 0:SSM_STATES]
        xi = st_scr[:, SSM_STATES:2 * SSM_STATES]
        u_next = jnp.swapaxes(un_ref[...], 0, 1).reshape(tc * nb, SSM_WIDTH).astype(BF16)
        y = d_ref[...] * jnp.swapaxes(up_ref[...], 0, 1).reshape(tc * nb, SSM_WIDTH)
        for j in range(n_blk):
            cols = slice(j * col_blk, (j + 1) * col_blk)
            y = y + _dot(buf_scr[b_out, :, cols].astype(BF16), cmat_ref[cols, :])
            for t in range(j * S5_BLOCK, (j + 1) * S5_BLOCK):
                r = t * nb
                nxr = ar * xr - ai * xi + buf_scr[b_scan, r:r + nb, 0:SSM_STATES]
                nxi = ar * xi + ai * xr + buf_scr[b_scan, r:r + nb, SSM_STATES:2 * SSM_STATES]
                buf_scr[b_scan, r:r + nb, 0:SSM_STATES] = nxr
                buf_scr[b_scan, r:r + nb, SSM_STATES:2 * SSM_STATES] = nxi
                xr, xi = nxr, nxi
            buf_scr[b_in, :, cols] = _dot(u_next, bmat_ref[:, cols])
        st_scr[:, 0:SSM_STATES] = xr
        st_scr[:, SSM_STATES:2 * SSM_STATES] = xi
        z = _gelu_tanh(y)
        out = z * _sigmoid(_dot(z.astype(BF16), wglu_ref[...]) + bglu_ref[...])
        o_ref[...] = jnp.swapaxes(out.reshape(tc, nb, SSM_WIDTH), 0, 1).astype(BF16)

    for r in range(3):
        @pl.when(g % 3 == r)
        def _(r=r):
            stages(r, (r + 2) % 3, (r + 1) % 3)


def s5_mixer(u, prm, l):
    nb, seq, _ = u.shape
    tc = S5_CHUNK
    nc = seq // tc
    assert nb == SUBLANES

    def chunk(fn):
        return pl.BlockSpec((nb, tc, SSM_WIDTH), lambda g: (0, fn(g), 0))

    names = ("s5_b", "s5_c", "s5_a", "ssm_d", "w_glu", "b_glu")
    return pl.pallas_call(
        functools.partial(_s5_kernel, tc=tc),
        out_shape=jax.ShapeDtypeStruct(u.shape, BF16),
        grid=(nc + 2,),
        in_specs=[chunk(lambda g: jnp.minimum(g, nc - 1)), chunk(lambda g: jnp.clip(g - 2, 0, nc - 1))]
        + [_layer_spec(prm[k], l) for k in names],
        out_specs=chunk(lambda g: jnp.clip(g - 2, 0, nc - 1)),
        scratch_shapes=[
            pltpu.VMEM((3, tc * nb, 2 * SSM_STATES), F32),
            pltpu.VMEM((nb, 2 * SSM_STATES), F32),
        ],
        compiler_params=_params(("arbitrary",), 48),
        name="s5_mixer",
    )(u, u, *[prm[k] for k in names])


def _ret_kernel(q_ref, k_ref, v_ref, g_ref, gn_ref, o_ref, dmat_scr, decq_scr, deck_scr, sdec_scr, st_scr, *, cr):
    pairs = RET_HEADS // 2
    hd = RET_HEAD_DIM
    scale = hd ** -0.5
    lane = lax.broadcasted_iota(jnp.int32, (1, LANES), 1)
    lo = lane < hd
    rows = lax.broadcasted_iota(jnp.int32, (LANES, LANES), 0) < hd
    cols = lax.broadcasted_iota(jnp.int32, (LANES, LANES), 1) < hd
    same = rows == cols

    @pl.when(pl.program_id(0) == 0)
    def _():
        t = lax.broadcasted_iota(jnp.int32, (cr, cr), 0)
        s = lax.broadcasted_iota(jnp.int32, (cr, cr), 1)
        dist = (t - s).astype(F32)
        pos = lax.broadcasted_iota(jnp.int32, (cr, LANES), 0).astype(F32)
        for h in range(RET_HEADS):
            lg = float(RET_LOG_GAMMA[h])
            dmat_scr[h // 2, :, (h % 2) * cr:(h % 2 + 1) * cr] = (
                jnp.where(dist >= 0, jnp.exp(jnp.maximum(dist, 0.0) * lg), 0.0) * scale)
        for j in range(pairs):
            lga, lgb = float(RET_LOG_GAMMA[2 * j]), float(RET_LOG_GAMMA[2 * j + 1])
            lg = jnp.where(lo, lga, lgb)
            decq_scr[j] = jnp.exp((pos + 1.0) * lg)
            deck_scr[j] = jnp.exp((cr - 1.0 - pos) * lg) * scale
            sdec_scr[j] = jnp.where(same, jnp.where(rows, math.exp(cr * lga), math.exp(cr * lgb)), 0.0)

    st_scr[...] = jnp.zeros_like(st_scr)

    avg2 =jnp.where(jnp.concatenate([same, same], axis=0), 1.0 / hd, 0.0).astype(BF16)
    blockmask = jnp.where(same, 1.0, 0.0).astype(F32)

    def group_mean(x):
        hi = x.astype(BF16)
        lo_part = (x - hi.astype(F32)).astype(BF16)
        return _dot(jnp.concatenate([hi, lo_part], axis=1), avg2)

    sls = [slice(j * LANES, (j + 1) * LANES) for j in range(pairs)]

    def chunk(c, carry):
        rows = pl.ds(pl.multiple_of(c * cr, cr), cr)
        scores = []
        for j in range(pairs):
            k = k_ref[rows, sls[j]]
            zero = jnp.zeros_like(k)
            k2 = jnp.concatenate([jnp.where(lo, k, zero), jnp.where(lo, zero, k)], axis=0)
            scores.append(_dot_nt(q_ref[rows, sls[j]], k2))
        probs = [(scores[j] * dmat_scr[j]).astype(BF16) for j in range(pairs)]
        outs = []
        for j in range(pairs):
            q = q_ref[rows, sls[j]]
            k = k_ref[rows, sls[j]]
            v = v_ref[rows, sls[j]]
            zero = jnp.zeros_like(v)
            v2 = jnp.concatenate([jnp.where(lo, v, zero), jnp.where(lo, zero, v)], axis=0)
            st = st_scr[j]
            cross = _dot((q.astype(F32) * decq_scr[j]).astype(BF16), st.astype(BF16))
            kdt = (k.astype(F32) * deck_scr[j]).T.astype(BF16)
            st_scr[j] = st * sdec_scr[j] + _dot(kdt, v) * blockmask
            outs.append(_dot(probs[j], v2) + cross)
        devs = [o - group_mean(o) for o in outs]
        variances = [group_mean(d * d) for d in devs]
        for j in range(pairs):
            on = devs[j] * lax.rsqrt(variances[j] + EPS) * gn_ref[:, sls[j]]
            gate = g_ref[rows, sls[j]].astype(F32)
            o_ref[rows, sls[j]] = (gate * _sigmoid(gate) * on).astype(BF16)
        return carry

    lax.fori_loop(0, q_ref.shape[0] // cr, chunk, 0)


def retention(ret, prm, l, nb, seq):
    n = ret.shape[0]
    cr = RET_CHUNK
    pairs = RET_HEADS // 2

    def part(col):
        return pl.BlockSpec((seq, RET_WIDTH), lambda b: (b, col))

    return pl.pallas_call(
        functools.partial(_ret_kernel, cr=cr),
        out_shape=jax.ShapeDtypeStruct((n, RET_WIDTH), BF16),
        grid=(nb,),
        in_specs=[part(0), part(1), part(2), part(3), _layer_spec(prm["gn"], l)],
        out_specs=pl.BlockSpec((seq, RET_WIDTH), lambda b: (b, 0)),
        scratch_shapes=[
            pltpu.VMEM((pairs, cr, 2 * cr), F32),
            pltpu.VMEM((pairs, cr, LANES), F32),
            pltpu.VMEM((pairs, cr, LANES), F32),
            pltpu.VMEM((pairs, LANES, LANES), F32),
            pltpu.VMEM((pairs, LANES, LANES), F32),
        ],
        compiler_params=_params(("arbitrary",), 40),
        name="retention",
    )(ret, ret, ret, ret, prm["gn"])


def _split_bf16(x, parts=3):
    out = []
    rem = np.float32(x)
    for _ in range(parts):
        hi = np.float32(rem.astype(BF16))
        out.append(float(hi))
        rem = np.float32(rem - hi)
    return out


def _diff_kernel(lq1_ref, lk1_ref, lq2_ref, lk2_ref, qt_ref, k_ref, vt_ref, g_ref, o_ref,
                 acc_scr, m_scr, s_scr, mt_scr, qx_scr, *, tile, lambda_init):
    j = pl.program_id(1)
    nq = qt_ref.shape[0]
    hv = DIFF_V_DIM
    dq = DIFF_QK_DIM
    ones_rows = 16
    combos = 4
    n_split = 3

    lam = (jnp.exp(jnp.sum(lq1_ref[...] * lk1_ref[...], axis=-1, keepdims=True))
           - jnp.exp(jnp.sum(lq2_ref[...] * lk2_ref[...], axis=-1, keepdims=True)) + lambda_init)

    def pick(vals):
        return jnp.where(j == 0, vals[0], jnp.where(j == 1, vals[1], vals[2]))

    slope2 = [float(np.float32(ALIBI_SLOPES[h]) * np.float32(LOG2E)) for h in range(DIFF_HEADS)]
    parts = [_split_bf16(s, n_split) for s in slope2]
    slopes = [pick([slope2[2 * jj + hl] for jj in range(3)]) for hl in range(2)]
    pieces = [[pick([parts[2 * jj + hl][n] for jj in range(3)]) for n in range(n_split)] for hl in range(2)]

    s_loc = lax.broadcasted_iota(jnp.int32, (tile, tile), 0)
    t_loc = lax.broadcasted_iota(jnp.int32, (tile, tile), 1)
    causal = s_loc <= t_loc

    row_t = lax.broadcasted_iota(jnp.int32, (LANES, tile), 0)
    lane_t = lax.broadcasted_iota(jnp.int32, (tile, LANES), 1)
    qfeat = []
    for hl in range(2):
        f = jnp.zeros((LANES, tile), F32)
        for n in range(n_split):
            f = jnp.where(row_t == n, pieces[hl][n], f)
        qfeat += [f.astype(BF16)] * 2
    qfeat = jnp.concatenate(qfeat, axis=1)
    for i in range(nq):
        qt = qt_ref[i]

        def own_rows(c):
            pieces_c = [jnp.zeros((dq * c, tile), BF16), qt[dq * c:dq * (c + 1)],
                        jnp.zeros((LANES - dq * (c + 1), tile), BF16)]
            return jnp.concatenate([x for x in pieces_c if x.shape[0]], axis=0)

        qx_scr[i, 0:LANES, :] = jnp.concatenate([own_rows(c) for c in range(combos)], axis=1)
        qx_scr[i, LANES:2 * LANES, :] = qfeat
    pos = lax.broadcasted_iota(jnp.int32, (tile, LANES), 0).astype(F32)
    kfeat = jnp.where(lane_t < n_split, pos, 0.0).astype(BF16)
    ones = jnp.ones((ones_rows, tile), BF16)

    def score(i, t, buf, c):
        cols = slice(c * tile, (c + 1) * tile)
        k_ext = jnp.concatenate([k_ref[t * tile:(t + 1) * tile, :], kfeat], axis=1)
        s_c = _dot(k_ext, qx_scr[i, :, cols])
        if t == i:
            s_c = jnp.where(causal, s_c, NEG_BIG)
        s_scr[buf, :, cols] = s_c
        mt_scr[buf, c] = jnp.max(s_c, axis=0, keepdims=True)

    def softmax(i, t, buf, first, c):
        shift = slopes[c // 2] * float((t - i) * tile)
        m_tile = mt_scr[buf, c] + shift
        if first:
            m_new = m_tile
            alpha = None
        else:
            m_old = m_scr[i % 2, c]
            m_new = jnp.maximum(m_old, m_tile)
            alpha = jnp.exp2(m_old - m_new)
        m_scr[i % 2, c] = m_new
        return jnp.exp2(s_scr[buf, :, c * tile:(c + 1) * tile] - (m_new - shift)).astype(BF16), alpha

    def accumulate(i, t, hl, ps, alphas, first):
        vaug = jnp.concatenate([vt_ref[t, hl * hv:(hl + 1) * hv, :], ones], axis=0)
        pv = _dot(vaug, jnp.concatenate(ps, axis=1))
        if first:
            acc_scr[i % 2, hl] = pv
        else:
            acc_scr[i % 2, hl] = acc_scr[i % 2, hl] * jnp.concatenate(alphas, axis=1) + pv

    def finalize(i):
        outs = []
        for hl in range(2):
            a0 = acc_scr[i % 2, hl, :, 0:tile]
            a1 = acc_scr[i % 2, hl, :, tile:2 * tile]
            o = a0[0:hv] / a0[hv:hv + 1] - lam * (a1[0:hv] / a1[hv:hv + 1])
            outs.append(o * lax.rsqrt(jnp.mean(o * o, axis=0, keepdims=True) + EPS))
        ot = jnp.concatenate(outs, axis=0)
        o_ref[i * tile:(i + 1) * tile, :] = (ot.T * g_ref[...] * (1.0 - lambda_init)).astype(BF16)

    visits = [(i, t) for i in range(nq) for t in [i] + list(range(i))]
    for c in range(combos):
        score(*visits[0], 0, c)
    for n, (i, t) in enumerate(visits):
        buf = n % 2
        first = t == i
        ps, alphas = [], []
        for c in range(combos):
            if n + 1 < len(visits):
                score(*visits[n + 1], 1 - buf, c)
            p, alpha = softmax(i, t, buf, first, c)
            ps.append(p)
            alphas.append(alpha)
            if c % 2 == 1:
                accumulate(i, t, c // 2, ps[-2:], alphas[-2:], first)
        if n + 1 == len(visits) or visits[n + 1][0] != i:
            finalize(i)


def diff_attention(qt, dk, vt, prm, l, lambda_init, nb, seq):
    n = dk.shape[0]
    tile = ATT_TILE
    assert tile == QV_TILE
    nq = seq // tile
    pairs = DIFF_HEADS // 2
    lam_names = ("lq1", "lk1", "lq2", "lk2")
    slabs = pl.BlockSpec((nq, LANES, tile), lambda b, j: (b, j, 0))
    return pl.pallas_call(
        functools.partial(_diff_kernel, tile=tile, lambda_init=lambda_init),
        out_shape=jax.ShapeDtypeStruct((n, DIFF_WIDTH), BF16),
        grid=(nb, pairs),
        in_specs=[_layer_spec(prm[k], l) for k in lam_names] + [
            slabs,
            pl.BlockSpec((seq, LANES), lambda b, j: (b, j)),
            slabs,
            pl.BlockSpec((None, 1, LANES), lambda b, j: (l, 0, j)),
        ],
        out_specs=pl.BlockSpec((seq, LANES), lambda b, j: (b, j)),
        scratch_shapes=[
            pltpu.VMEM((2, 2, DIFF_V_DIM + 16, 2 * tile), F32),
            pltpu.VMEM((2, 4, 1, tile), F32),
            pltpu.VMEM((2, tile, 4 * tile), F32),
            pltpu.VMEM((2, 4, 1, tile), F32),
            pltpu.VMEM((nq, 2 * LANES, 4 * tile), BF16),
        ],
        compiler_params=_params(("parallel", "parallel"), 40),
        name="diff_attention",
    )(*[prm[k] for k in lam_names], qt, dk, vt, prm["subln"])


def _mlp_kernel(h_ref, s_ref, r_ref, d_ref, wo_ref, g2_ref, wup_ref, cw_ref, cb_ref, wdn_ref, o_ref,
                carry_scr, *, tm, tiles_per_seq):
    @pl.when(pl.program_id(0) % tiles_per_seq == 0)
    def _():
        carry_scr[...] = jnp.zeros_like(carry_scr)

    mix = jnp.concatenate([s_ref[...], r_ref[...], d_ref[...]], axis=1)
    h1 = h_ref[...] + _dot(mix, wo_ref[...])
    o_ref[...] = h1
    hn = _rms(h1, g2_ref[...]).astype(BF16)
    row8 =lax.broadcasted_iota(jnp.int32, (SUBLANES, FF_CHUNK), 0)
    n_chunks = D_FF // FF_CHUNK

    def up(jj):
        return [_dot(hn, wup_ref[:, c0:c0 + FF_CHUNK]) for c0 in (jj * FF_CHUNK, D_FF + jj * FF_CHUNK)]

    def conv(u, c0):
        cols = slice(c0, c0 + FF_CHUNK)
        c6 = carry_scr[6:7, cols]
        c7 = carry_scr[7:8, cols]
        r1 = pltpu.roll(u, 1, 0)
        r2 = pltpu.roll(u, 2, 0)
        p1 = jnp.concatenate([jnp.where(row8 == 0, c7, r1[0:8]), r1[8:]], axis=0)
        p2 = jnp.concatenate([jnp.where(row8 == 0, c6, jnp.where(row8 == 1, c7, r2[0:8])), r2[8:]], axis=0)
        carry_scr[:, cols] = u[tm - 8:tm, :]
        w = cw_ref[:, cols]
        return w[0:1] * p2 + w[1:2] * p1 + w[2:3] * u + cb_ref[:, cols]

    acc = jnp.zeros((tm, D_MODEL), F32)
    u_cur = up(0)
    acts = []
    for jj in range(n_chunks):
        u_next = up(jj + 1) if jj + 1 < n_chunks else None
        a = conv(u_cur[0], jj * FF_CHUNK)
        b = conv(u_cur[1], D_FF + jj * FF_CHUNK)
        acts.append((a * _sigmoid(a) * b).astype(BF16))
        if len(acts) == DOWN_GROUP or jj + 1 == n_chunks:
            first = jj + 1 - len(acts)
            acc = acc + _dot(jnp.concatenate(acts, axis=1), wdn_ref[first * FF_CHUNK:(jj + 1) * FF_CHUNK, :])
            acts = []
        u_cur = u_next
    o_ref[...] += acc


def mixer_mlp(h, s_out, r_out, d_out, prm, l, seq):
    n = h.shape[0]
    tm = TM_MLP

    def rows_in(width):
        return pl.BlockSpec((tm, width), lambda i: (i, 0))

    names = ("w_out", "g2", "w_up", "conv_w", "conv_b", "w_down")
    return pl.pallas_call(
        functools.partial(_mlp_kernel, tm=tm, tiles_per_seq=seq // tm),
        out_shape=jax.ShapeDtypeStruct((n, D_MODEL), F32),
        grid=(n // tm,),
        in_specs=[rows_in(D_MODEL), rows_in(SSM_WIDTH), rows_in(RET_WIDTH), rows_in(DIFF_WIDTH)]
        + [_layer_spec(prm[k], l, single=True) for k in names],
        out_specs=rows_in(D_MODEL),
        scratch_shapes=[pltpu.VMEM((SUBLANES, 2 * D_FF), F32)],
        compiler_params=_params(("arbitrary",), 56),
        name="mixer_mlp",
    )(h, s_out, r_out, d_out, *[prm[k] for k in names])


def _ple(h_ref, p_ref, g3_ref, wpg_ref, wpe_ref):
    h2 = h_ref[...]
    gate = _sigmoid(_dot(_rms(h2, g3_ref[...]).astype(BF16), wpg_ref[...]))
    return h2 + gate * _dot(p_ref[...].astype(BF16), wpe_ref[...])


def _ple_proj_kernel(h_ref, p_ref, g3_ref, wpg_ref, wpe_ref, g1_ref, w_ref,
                     ho_ref, u_ref, ret_ref, dk_ref, qt_ref, vt_ref):
    h3 = _ple(h_ref, p_ref, g3_ref, wpg_ref, wpe_ref)
    ho_ref[...] = h3
    hn = _rms(h3, g1_ref[...]).astype(BF16)
    _project(hn, w_ref, u_ref, ret_ref, dk_ref, qt_ref, vt_ref)


def _ple_final_kernel(h_ref, p_ref, g3_ref, wpg_ref, wpe_ref, gf_ref, o_ref):
    o_ref[...] = _rms(_ple(h_ref, p_ref, g3_ref, wpg_ref, wpe_ref), gf_ref[...])


def _ple_in_specs(n, tm, l, prm):
    return [pl.BlockSpec((tm, D_MODEL), lambda i: (i, 0)), pl.BlockSpec((tm, PLE_DIM), lambda i: (l * (n // tm) + i, 0)),
            _layer_spec(prm["g3"], l), _layer_spec(prm["w_pg"], l), _layer_spec(prm["w_pe"], l)]


def ple_proj(h, p_flat, prm, l):
    n = h.shape[0]
    tm = TM_PROJ
    shapes, specs = _proj_out(n, tm)
    return pl.pallas_call(
        _ple_proj_kernel,
        out_shape=(jax.ShapeDtypeStruct((n, D_MODEL), F32),) + shapes,
        grid=(n // tm,),
        in_specs=_ple_in_specs(n, tm, l, prm) + [_layer_spec(prm["g1"], l + 1), _layer_spec(prm["w_in"], l + 1)],
        out_specs=(pl.BlockSpec((tm, D_MODEL), lambda i: (i, 0)),) + specs,
        compiler_params=_params(("parallel",), 56),
        name="ple_proj",
    )(h, p_flat, prm["g3"], prm["w_pg"], prm["w_pe"], prm["g1"], prm["w_in"])


def ple_final(h, p_flat, prm, l, final_g):
    n = h.shape[0]
    tm = TM_PROJ
    return pl.pallas_call(
        _ple_final_kernel,
        out_shape=jax.ShapeDtypeStruct((n, D_MODEL), F32),
        grid=(n // tm,),
        in_specs=_ple_in_specs(n, tm, l, prm) + [pl.BlockSpec(final_g.shape, lambda i: (0, 0))],
        out_specs=pl.BlockSpec((tm, D_MODEL), lambda i: (i, 0)),
        compiler_params=_params(("parallel",), 40),
        name="ple_final",
    )(h, p_flat, prm["g3"], prm["w_pg"], prm["w_pe"], final_g)


def kernel(x, p, norm1_g, w_in, ssm_lam_re, ssm_lam_im, ssm_log_dt, ssm_b_re, ssm_b_im, ssm_c_re, ssm_c_im, ssm_d, ssm_w_glu, ssm_b_glu, ret_gn_g, diff_lq1, diff_lk1, diff_lq2, diff_lk2, diff_subln_g, w_out, norm2_g, w_up, conv_w, conv_b, w_down, norm3_g, w_pg, w_pe, final_g):
    nb, seq, _ = x.shape
    depth = w_in.shape[0]
    n = nb * seq
    assert seq % max(RET_CHUNK, ATT_TILE, S5_CHUNK, TM_MLP) == 0 and n % TM_PROJ == 0 and w_in.shape[2] == PROJ_WIDTH

    def row(a):
        return a[:, None, :]

    bmat, cmat, s5_a = s5_prep(ssm_lam_re, ssm_lam_im, ssm_log_dt, ssm_b_re, ssm_b_im, ssm_c_re, ssm_c_im)
    prm = {
        "w_in": w_in.astype(BF16), "w_out": w_out.astype(BF16), "w_up": w_up.astype(BF16),
        "w_down": w_down.astype(BF16), "w_pg": w_pg.astype(BF16), "w_pe": w_pe.astype(BF16),
        "w_glu": ssm_w_glu.astype(BF16), "s5_b": bmat.astype(BF16), "s5_c": cmat.astype(BF16), "s5_a": s5_a,
        "g1": row(norm1_g), "g2": row(norm2_g), "g3": row(norm3_g), "ssm_d": row(ssm_d), "b_glu": row(ssm_b_glu),
        "gn": row(ret_gn_g), "subln": row(diff_subln_g), "conv_w": conv_w, "conv_b": row(conv_b),
        "lq1": row(diff_lq1), "lk1": row(diff_lk1), "lq2": row(diff_lq2), "lk2": row(diff_lk2),
    }
    p_flat = p.reshape(depth * n, PLE_DIM)

    h = x.reshape(n, D_MODEL)
    u, ret, dk, qt, vt = norm_proj(h, prm, 0)
    out = None
    for l in range(depth):
        lambda_init = 0.8 - 0.6 * math.exp(-0.3 * l)
        s_out = s5_mixer(u.reshape(nb, seq, SSM_WIDTH), prm, l)
        r_out = retention(ret, prm, l, nb, seq)
        d_out = diff_attention(qt, dk, vt, prm, l, lambda_init, nb, seq)
        h = mixer_mlp(h, s_out.reshape(n, SSM_WIDTH), r_out, d_out, prm, l, seq)
        if l + 1 < depth:
            h, u, ret, dk, qt, vt = ple_proj(h, p_flat, prm, l)
        else:
            out = ple_final(h, p_flat, prm, l, final_g[None, :])
    return out.reshape(nb, seq, D_MODEL)
```

```python
import functools
import math

import jax
import jax.numpy as jnp
import numpy as np
from jax import lax
from jax.experimental import pallas as pl
from jax.experimental.pallas import tpu as pltpu

F32 = jnp.float32
BF16 = jnp.bfloat16

D_MODEL = 1024
PLE_DIM = 256
SSM_WIDTH = 256
SSM_GROUP = 16
SSM_GROUPS = 16
SSM_STATE = 64
SSM_STATES = SSM_GROUPS * SSM_STATE
RET_HEADS = 6
RET_HEAD_DIM = 64
RET_WIDTH = RET_HEADS * RET_HEAD_DIM
DIFF_HEADS = 6
DIFF_QK_DIM = 32
DIFF_V_DIM = 64
DIFF_WIDTH = DIFF_HEADS * DIFF_V_DIM
D_FF = 2816
EPS = 1e-6
RET_LOG_GAMMA = np.log1p(-(2.0 ** (-5.0 - np.arange(RET_HEADS)))).astype(np.float32)
ALIBI_SLOPES = (2.0 ** (-8.0 * (np.arange(DIFF_HEADS) + 1) / DIFF_HEADS)).astype(np.float32)
PROJ_COLS = np.cumsum([0, SSM_WIDTH, 4 * RET_WIDTH, DIFF_WIDTH, DIFF_WIDTH, DIFF_WIDTH])
PROJ_WIDTH = int(PROJ_COLS[-1])

LANES = 128
SUBLANES = 8
VMEM_BYTES_V7X = 64 * 1024 * 1024
NEG_BIG = -1e30
LOG2E = math.log2(math.e)

TM_PROJ = 1024
TM_MLP = 256
FF_CHUNK = 256
DOWN_GROUP = 11
S5_CHUNK = 128
S5_BLOCK = 16
RET_CHUNK = 256
ATT_TILE = 256
QV_TILE = 256


def _params(semantics, vmem_mib):
    assert vmem_mib * 1024 * 1024 < VMEM_BYTES_V7X
    return pltpu.CompilerParams(dimension_semantics=semantics, vmem_limit_bytes=vmem_mib * 1024 * 1024)


def _layer_spec(arr, l, single=False):
    nd = arr.ndim
    kw = {"pipeline_mode": pl.Buffered(1)} if single else {}
    return pl.BlockSpec((None,) + arr.shape[1:], lambda *_: (l,) + (0,) * (nd - 1), **kw)


def _rms(x, g):
    return x * lax.rsqrt(jnp.mean(x * x, axis=-1, keepdims=True) + EPS) * g


def _sigmoid(x):
    return 1.0 / (1.0 + jnp.exp2(x * -LOG2E))


def _dot(a, b):
    return jnp.dot(a, b, preferred_element_type=F32)


def _dot_nt(a, b):
    return lax.dot_general(a, b, (((1,), (1,)), ((), ())), preferred_element_type=F32)


def _project(hn, w_ref, u_ref, ret_ref, dk_ref, qt_ref, vt_ref):
    c = [int(v) for v in PROJ_COLS]

    def cols(a, b):
        return _dot(hn, w_ref[:, c[a]:c[b]].astype(BF16))

    u_ref[...] = cols(0, 1)
    ret_ref[...] = cols(1, 2).astype(BF16)
    qk = cols(2, 4)
    dk_ref[...] = qk[:, DIFF_WIDTH:].astype(BF16)
    q = qk[:, :DIFF_WIDTH] * (DIFF_QK_DIM ** -0.5 * LOG2E)
    v = cols(4, 5)
    for r in range(qt_ref.shape[0]):
        rows = slice(r * QV_TILE, (r + 1) * QV_TILE)
        qt_ref[r] = q[rows, :].T.astype(BF16)
        vt_ref[r] = v[rows, :].T.astype(BF16)


def _norm_proj_kernel(h_ref, g_ref, w_ref, u_ref, ret_ref, dk_ref, qt_ref, vt_ref):
    hn = _rms(h_ref[...], g_ref[...]).astype(BF16)
    _project(hn, w_ref, u_ref, ret_ref, dk_ref, qt_ref, vt_ref)


def _proj_out(n, tm):
    slab = jax.ShapeDtypeStruct((n // QV_TILE, DIFF_WIDTH, QV_TILE), BF16)
    shapes = (
        jax.ShapeDtypeStruct((n, SSM_WIDTH), F32),
        jax.ShapeDtypeStruct((n, 4 * RET_WIDTH), BF16),
        jax.ShapeDtypeStruct((n, DIFF_WIDTH), BF16),
        slab,
        slab,
    )
    slab_spec = pl.BlockSpec((tm // QV_TILE, DIFF_WIDTH, QV_TILE), lambda i: (i, 0, 0))
    specs = (
        pl.BlockSpec((tm, SSM_WIDTH), lambda i: (i, 0)),
        pl.BlockSpec((tm, 4 * RET_WIDTH), lambda i: (i, 0)),
        pl.BlockSpec((tm, DIFF_WIDTH), lambda i: (i, 0)),
        slab_spec,
        slab_spec,
    )
    return shapes, specs


def norm_proj(h, prm, l):
    n = h.shape[0]
    tm = TM_PROJ
    shapes, specs = _proj_out(n, tm)
    return pl.pallas_call(
        _norm_proj_kernel,
        out_shape=shapes,
        grid=(n // tm,),
        in_specs=[pl.BlockSpec((tm, D_MODEL), lambda i: (i, 0)), _layer_spec(prm["g1"], l),
                  _layer_spec(prm["w_in"], l, single=True)],
        out_specs=specs,
        compiler_params=_params(("parallel",), 48),
        name="norm_proj",
    )(h, prm["g1"], prm["w_in"])


def _s5_prep_kernel(lr_ref, li_ref, ldt_ref, brt_ref, bit_ref, crt_ref, cit_ref, b_ref, c_ref, a_ref):
    lr = lr_ref[0]
    li = li_ref[0]
    dt = jnp.exp(ldt_ref[0])
    mag = jnp.exp(lr * dt)
    ar = mag * jnp.cos(li * dt)
    ai = mag * jnp.sin(li * dt)
    den = lr * lr + li * li
    cr = ((ar - 1.0) * lr + ai * li) / den
    ci = (ai * lr - (ar - 1.0) * li) / den
    b_ref[...] = jnp.zeros_like(b_ref)
    c_ref[...] = jnp.zeros_like(c_ref)
    for g in range(SSM_GROUPS):
        rows = slice(g * SSM_GROUP, (g + 1) * SSM_GROUP)
        cols = slice(g * SSM_STATE, (g + 1) * SSM_STATE)
        cols_im = slice(SSM_STATES + g * SSM_STATE, SSM_STATES + (g + 1) * SSM_STATE)
        crg = cr[g:g + 1, :]
        cig = ci[g:g + 1, :]
        br = brt_ref[0, g]
        bi = bit_ref[0, g]
        b_ref[0, rows, cols] = crg * br - cig * bi
        b_ref[0, rows, cols_im] = crg * bi + cig * br
        c_ref[0, cols, rows] = crt_ref[0, g]
        c_ref[0, cols_im, rows] = -cit_ref[0, g]
        a_ref[0, 0:1, cols] = ar[g:g + 1, :]
        a_ref[0, 1:2, cols] = ai[g:g + 1, :]


def s5_prep(lam_re, lam_im, log_dt, b_re, b_im, c_re, c_im):
    depth = lam_re.shape[0]
    brt = jnp.transpose(b_re, (0, 1, 3, 2))
    bit = jnp.transpose(b_im, (0, 1, 3, 2))
    crt = jnp.transpose(c_re, (0, 1, 3, 2))
    cit = jnp.transpose(c_im, (0, 1, 3, 2))
    ldt = log_dt[..., None]

    def spec(a):
        nd = a.ndim
        return pl.BlockSpec((1,) + a.shape[1:], lambda l: (l,) + (0,) * (nd - 1))

    ins = (lam_re, lam_im, ldt, brt, bit, crt, cit)
    out_shape = (
        jax.ShapeDtypeStruct((depth, SSM_WIDTH, 2 * SSM_STATES), F32),
        jax.ShapeDtypeStruct((depth, 2 * SSM_STATES, SSM_WIDTH), F32),
        jax.ShapeDtypeStruct((depth, 2, SSM_STATES), F32),
    )
    return pl.pallas_call(
        _s5_prep_kernel,
        out_shape=out_shape,
        grid=(depth,),
        in_specs=[spec(a) for a in ins],
        out_specs=tuple(spec(o) for o in out_shape),
        compiler_params=_params(("parallel",), 32),
        name="s5_prep",
    )(*ins)


def _gelu_tanh(x):
    return 0.5 * x * (1.0 + jnp.tanh(math.sqrt(2.0 / math.pi) * (x + 0.044715 * (x * x * x))))


def _s5_kernel(un_ref, up_ref, bmat_ref, cmat_ref, a_ref, d_ref, wglu_ref, bglu_ref, o_ref,
               buf_scr, st_scr, *, tc):
    nb = un_ref.shape[0]
    g = pl.program_id(0)
    n_blk = tc // S5_BLOCK
    col_blk = 2 * SSM_STATES // n_blk

    @pl.when(g == 0)
    def _():
        buf_scr[...] = jnp.zeros_like(buf_scr)

    @pl.when(g <= 1)
    def _():
        st_scr[...] = jnp.zeros_like(st_scr)

    def stages(b_in, b_scan, b_out):
        ar = jnp.broadcast_to(a_ref[0:1, :], (nb, SSM_STATES))
        ai = jnp.broadcast_to(a_ref[1:2, :], (nb, SSM_STATES))
        xr = st_scr[:, 0:SSM_STATES]
        xi = st_scr[:, SSM_STATES:2 * SSM_STATES]
        u_next = jnp.swapaxes(un_ref[...], 0, 1).reshape(tc * nb, SSM_WIDTH).astype(BF16)
        y = d_ref[...] * jnp.swapaxes(up_ref[...], 0, 1).reshape(tc * nb, SSM_WIDTH)
        for j in range(n_blk):
            cols = slice(j * col_blk, (j + 1) * col_blk)
            y = y + _dot(buf_scr[b_out, :, cols].astype(BF16), cmat_ref[cols, :])
            for t in range(j * S5_BLOCK, (j + 1) * S5_BLOCK):
                r = t * nb
                nxr = ar * xr - ai * xi + buf_scr[b_scan, r:r + nb, 0:SSM_STATES]
                nxi = ar * xi + ai * xr + buf_scr[b_scan, r:r + nb, SSM_STATES:2 * SSM_STATES]
                buf_scr[b_scan, r:r + nb, 0:SSM_STATES] = nxr
                buf_scr[b_scan, r:r + nb, SSM_STATES:2 * SSM_STATES] = nxi
                xr, xi = nxr, nxi
            buf_scr[b_in, :, cols] = _dot(u_next, bmat_ref[:, cols])
        st_scr[:, 0:SSM_STATES] = xr
        st_scr[:, SSM_STATES:2 * SSM_STATES] = xi
        z = _gelu_tanh(y)
        out = z * _sigmoid(_dot(z.astype(BF16), wglu_ref[...]) + bglu_ref[...])
        o_ref[...] = jnp.swapaxes(out.reshape(tc, nb, SSM_WIDTH), 0, 1).astype(BF16)

    for r in range(3):
        @pl.when(g % 3 == r)
        def _(r=r):
            stages(r, (r + 2) % 3, (r + 1) % 3)


def s5_mixer(u, prm, l):
    nb, seq, _ = u.shape
    tc = S5_CHUNK
    nc = seq // tc
    assert nb == SUBLANES

    def chunk(fn):
        return pl.BlockSpec((nb, tc, SSM_WIDTH), lambda g: (0, fn(g), 0))

    names = ("s5_b", "s5_c", "s5_a", "ssm_d", "w_glu", "b_glu")
    return pl.pallas_call(
        functools.partial(_s5_kernel, tc=tc),
        out_shape=jax.ShapeDtypeStruct(u.shape, BF16),
        grid=(nc + 2,),
        in_specs=[chunk(lambda g: jnp.minimum(g, nc - 1)), chunk(lambda g: jnp.clip(g - 2, 0, nc - 1))]
        + [_layer_spec(prm[k], l) for k in names],
        out_specs=chunk(lambda g: jnp.clip(g - 2, 0, nc - 1)),
        scratch_shapes=[
            pltpu.VMEM((3, tc * nb, 2 * SSM_STATES), F32),
            pltpu.VMEM((nb, 2 * SSM_STATES), F32),
        ],
        compiler_params=_params(("arbitrary",), 48),
        name="s5_mixer",
    )(u, u, *[prm[k] for k in names])


def _ret_kernel(q_ref, k_ref, v_ref, g_ref, gn_ref, o_ref, dmat_scr, decq_scr, deck_scr, sdec_scr, st_scr, *, cr):
    pairs = RET_HEADS // 2
    hd = RET_HEAD_DIM
    scale = hd ** -0.5
    lane = lax.broadcasted_iota(jnp.int32, (1, LANES), 1)
    lo = lane < hd
    rows = lax.broadcasted_iota(jnp.int32, (LANES, LANES), 0) < hd
    cols = lax.broadcasted_iota(jnp.int32, (LANES, LANES), 1) < hd
    same = rows == cols

    @pl.when(pl.program_id(0) == 0)
    def _():
        t = lax.broadcasted_iota(jnp.int32, (cr, cr), 0)
        s = lax.broadcasted_iota(jnp.int32, (cr, cr), 1)
        dist = (t - s).astype(F32)
        pos = lax.broadcasted_iota(jnp.int32, (cr, LANES), 0).astype(F32)
        for h in range(RET_HEADS):
            lg = float(RET_LOG_GAMMA[h])
            dmat_scr[h // 2, :, (h % 2) * cr:(h % 2 + 1) * cr] = (
                jnp.where(dist >= 0, jnp.exp(jnp.maximum(dist, 0.0) * lg), 0.0) * scale)
        for j in range(pairs):
            lga, lgb = float(RET_LOG_GAMMA[2 * j]), float(RET_LOG_GAMMA[2 * j + 1])
            lg = jnp.where(lo, lga, lgb)
            decq_scr[j] = jnp.exp((pos + 1.0) * lg)
            deck_scr[j] = jnp.exp((cr - 1.0 - pos) * lg) * scale
            sdec_scr[j] = jnp.where(same, jnp.where(rows, math.exp(cr * lga), math.exp(cr * lgb)), 0.0)

    st_scr[...] = jnp.zeros_like(st_scr)

    avg2 =jnp.where(jnp.concatenate([same, same], axis=0), 1.0 / hd, 0.0).astype(BF16)
    blockmask = jnp.where(same, 1.0, 0.0).astype(F32)

    def group_mean(x):
        hi = x.astype(BF16)
        lo_part = (x - hi.astype(F32)).astype(BF16)
        return _dot(jnp.concatenate([hi, lo_part], axis=1), avg2)

    sls = [slice(j * LANES, (j + 1) * LANES) for j in range(pairs)]

    def chunk(c, carry):
        rows = pl.ds(pl.multiple_of(c * cr, cr), cr)
        scores = []
        for j in range(pairs):
            k = k_ref[rows, sls[j]]
            zero = jnp.zeros_like(k)
            k2 = jnp.concatenate([jnp.where(lo, k, zero), jnp.where(lo, zero, k)], axis=0)
            scores.append(_dot_nt(q_ref[rows, sls[j]], k2))
        probs = [(scores[j] * dmat_scr[j]).astype(BF16) for j in range(pairs)]
        outs = []
        for j in range(pairs):
            q = q_ref[rows, sls[j]]
            k = k_ref[rows, sls[j]]
            v = v_ref[rows, sls[j]]
            zero = jnp.zeros_like(v)
            v2 = jnp.concatenate([jnp.where(lo, v, zero), jnp.where(lo, zero, v)], axis=0)
            st = st_scr[j]
            cross = _dot((q.astype(F32) * decq_scr[j]).astype(BF16), st.astype(BF16))
            kdt = (k.astype(F32) * deck_scr[j]).T.astype(BF16)
            st_scr[j] = st * sdec_scr[j] + _dot(kdt, v) * blockmask
            outs.append(_dot(probs[j], v2) + cross)
        devs = [o - group_mean(o) for o in outs]
        variances = [group_mean(d * d) for d in devs]
        for j in range(pairs):
            on = devs[j] * lax.rsqrt(variances[j] + EPS) * gn_ref[:, sls[j]]
            gate = g_ref[rows, sls[j]].astype(F32)
            o_ref[rows, sls[j]] = (gate * _sigmoid(gate) * on).astype(BF16)
        return carry

    lax.fori_loop(0, q_ref.shape[0] // cr, chunk, 0)


def retention(ret, prm, l, nb, seq):
    n = ret.shape[0]
    cr = RET_CHUNK
    pairs = RET_HEADS // 2

    def part(col):
        return pl.BlockSpec((seq, RET_WIDTH), lambda b: (b, col))

    return pl.pallas_call(
        functools.partial(_ret_kernel, cr=cr),
        out_shape=jax.ShapeDtypeStruct((n, RET_WIDTH), BF16),
        grid=(nb,),
        in_specs=[part(0), part(1), part(2), part(3), _layer_spec(prm["gn"], l)],
        out_specs=pl.BlockSpec((seq, RET_WIDTH), lambda b: (b, 0)),
        scratch_shapes=[
            pltpu.VMEM((pairs, cr, 2 * cr), F32),
            pltpu.VMEM((pairs, cr, LANES), F32),
            pltpu.VMEM((pairs, cr, LANES), F32),
            pltpu.VMEM((pairs, LANES, LANES), F32),
            pltpu.VMEM((pairs, LANES, LANES), F32),
        ],
        compiler_params=_params(("arbitrary",), 40),
        name="retention",
    )(ret, ret, ret, ret, prm["gn"])


def _split_bf16(x, parts=3):
    out = []
    rem = np.float32(x)
    for _ in range(parts):
        hi = np.float32(rem.astype(BF16))
        out.append(float(hi))
        rem = np.float32(rem - hi)
    return out


def _diff_kernel(lq1_ref, lk1_ref, lq2_ref, lk2_ref, qt_ref, k_ref, vt_ref, g_ref, o_ref,
                 acc_scr, m_scr, s_scr, mt_scr, qx_scr, *, tile, lambda_init):
    j = pl.program_id(1)
    nq = qt_ref.shape[0]
    hv = DIFF_V_DIM
    dq = DIFF_QK_DIM
    ones_rows = 16
    combos = 4
    n_split = 3

    lam = (jnp.exp(jnp.sum(lq1_ref[...] * lk1_ref[...], axis=-1, keepdims=True))
           - jnp.exp(jnp.sum(lq2_ref[...] * lk2_ref[...], axis=-1, keepdims=True)) + lambda_init)

    def pick(vals):
        return jnp.where(j == 0, vals[0], jnp.where(j == 1, vals[1], vals[2]))

    slope2 = [float(np.float32(ALIBI_SLOPES[h]) * np.float32(LOG2E)) for h in range(DIFF_HEADS)]
    parts = [_split_bf16(s, n_split) for s in slope2]
    slopes = [pick([slope2[2 * jj + hl] for jj in range(3)]) for hl in range(2)]
    pieces = [[pick([parts[2 * jj + hl][n] for jj in range(3)]) for n in range(n_split)] for hl in range(2)]

    s_loc = lax.broadcasted_iota(jnp.int32, (tile, tile), 0)
    t_loc = lax.broadcasted_iota(jnp.int32, (tile, tile), 1)
    causal = s_loc <= t_loc

    row_t = lax.broadcasted_iota(jnp.int32, (LANES, tile), 0)
    lane_t = lax.broadcasted_iota(jnp.int32, (tile, LANES), 1)
    qfeat = []
    for hl in range(2):
        f = jnp.zeros((LANES, tile), F32)
        for n in range(n_split):
            f = jnp.where(row_t == n, pieces[hl][n], f)
        qfeat += [f.astype(BF16)] * 2
    qfeat = jnp.concatenate(qfeat, axis=1)
    for i in range(nq):
        qt = qt_ref[i]

        def own_rows(c):
            pieces_c = [jnp.zeros((dq * c, tile), BF16), qt[dq * c:dq * (c + 1)],
                        jnp.zeros((LANES - dq * (c + 1), tile), BF16)]
            return jnp.concatenate([x for x in pieces_c if x.shape[0]], axis=0)

        qx_scr[i, 0:LANES, :] = jnp.concatenate([own_rows(c) for c in range(combos)], axis=1)
        qx_scr[i, LANES:2 * LANES, :] = qfeat
    pos = lax.broadcasted_iota(jnp.int32, (tile, LANES), 0).astype(F32)
    kfeat = jnp.where(lane_t < n_split, pos, 0.0).astype(BF16)
    ones = jnp.ones((ones_rows, tile), BF16)

    def score(i, t, buf, c):
        cols = slice(c * tile, (c + 1) * tile)
        k_ext = jnp.concatenate([k_ref[t * tile:(t + 1) * tile, :], kfeat], axis=1)
        s_c = _dot(k_ext, qx_scr[i, :, cols])
        if t == i:
            s_c = jnp.where(causal, s_c, NEG_BIG)
        s_scr[buf, :, cols] = s_c
        mt_scr[buf, c] = jnp.max(s_c, axis=0, keepdims=True)

    def softmax(i, t, buf, first, c):
        shift = slopes[c // 2] * float((t - i) * tile)
        m_tile = mt_scr[buf, c] + shift
        if first:
            m_new = m_tile
            alpha = None
        else:
            m_old = m_scr[i % 2, c]
            m_new = jnp.maximum(m_old, m_tile)
            alpha = jnp.exp2(m_old - m_new)
        m_scr[i % 2, c] = m_new
        return jnp.exp2(s_scr[buf, :, c * tile:(c + 1) * tile] - (m_new - shift)).astype(BF16), alpha

    def accumulate(i, t, hl, ps, alphas, first):
        vaug = jnp.concatenate([vt_ref[t, hl * hv:(hl + 1) * hv, :], ones], axis=0)
        pv = _dot(vaug, jnp.concatenate(ps, axis=1))
        if first:
            acc_scr[i % 2, hl] = pv
        else:
            acc_scr[i % 2, hl] = acc_scr[i % 2, hl] * jnp.concatenate(alphas, axis=1) + pv

    def finalize(i):
        outs = []
        for hl in range(2):
            a0 = acc_scr[i % 2, hl, :, 0:tile]
            a1 = acc_scr[i % 2, hl, :, tile:2 * tile]
            o = a0[0:hv] / a0[hv:hv + 1] - lam * (a1[0:hv] / a1[hv:hv + 1])
            outs.append(o * lax.rsqrt(jnp.mean(o * o, axis=0, keepdims=True) + EPS))
        ot = jnp.concatenate(outs, axis=0)
        o_ref[i * tile:(i + 1) * tile, :] = (ot.T * g_ref[...] * (1.0 - lambda_init)).astype(BF16)

    visits = [(i, t) for i in range(nq) for t in [i] + list(range(i))]
    for c in range(combos):
        score(*visits[0], 0, c)
    for n, (i, t) in enumerate(visits):
        buf = n % 2
        first = t == i
        ps, alphas = [], []
        for c in range(combos):
            if n + 1 < len(visits):
                score(*visits[n + 1], 1 - buf, c)
            p, alpha = softmax(i, t, buf, first, c)
            ps.append(p)
            alphas.append(alpha)
            if c % 2 == 1:
                accumulate(i, t, c // 2, ps[-2:], alphas[-2:], first)
        if n + 1 == len(visits) or visits[n + 1][0] != i:
            finalize(i)


def diff_attention(qt, dk, vt, prm, l, lambda_init, nb, seq):
    n = dk.shape[0]
    tile = ATT_TILE
    assert tile == QV_TILE
    nq = seq // tile
    pairs = DIFF_HEADS // 2
    lam_names = ("lq1", "lk1", "lq2", "lk2")
    slabs = pl.BlockSpec((nq, LANES, tile), lambda b, j: (b, j, 0))
    return pl.pallas_call(
        functools.partial(_diff_kernel, tile=tile, lambda_init=lambda_init),
        out_shape=jax.ShapeDtypeStruct((n, DIFF_WIDTH), BF16),
        grid=(nb, pairs),
        in_specs=[_layer_spec(prm[k], l) for k in lam_names] + [
            slabs,
            pl.BlockSpec((seq, LANES), lambda b, j: (b, j)),
            slabs,
            pl.BlockSpec((None, 1, LANES), lambda b, j: (l, 0, j)),
        ],
        out_specs=pl.BlockSpec((seq, LANES), lambda b, j: (b, j)),
        scratch_shapes=[
            pltpu.VMEM((2, 2, DIFF_V_DIM + 16, 2 * tile), F32),
            pltpu.VMEM((2, 4, 1, tile), F32),
            pltpu.VMEM((2, tile, 4 * tile), F32),
            pltpu.VMEM((2, 4, 1, tile), F32),
            pltpu.VMEM((nq, 2 * LANES, 4 * tile), BF16),
        ],
        compiler_params=_params(("parallel", "parallel"), 40),
        name="diff_attention",
    )(*[prm[k] for k in lam_names], qt, dk, vt, prm["subln"])


def _mlp_kernel(h_ref, s_ref, r_ref, d_ref, wo_ref, g2_ref, wup_ref, cw_ref, cb_ref, wdn_ref, o_ref,
                carry_scr, *, tm, tiles_per_seq):
    @pl.when(pl.program_id(0) % tiles_per_seq == 0)
    def _():
        carry_scr[...] = jnp.zeros_like(carry_scr)

    mix = jnp.concatenate([s_ref[...], r_ref[...], d_ref[...]], axis=1)
    h1 = h_ref[...] + _dot(mix, wo_ref[...])
    o_ref[...] = h1
    hn = _rms(h1, g2_ref[...]).astype(BF16)
    row8 =lax.broadcasted_iota(jnp.int32, (SUBLANES, FF_CHUNK), 0)
    n_chunks = D_FF // FF_CHUNK

    def up(jj):
        return [_dot(hn, wup_ref[:, c0:c0 + FF_CHUNK]) for c0 in (jj * FF_CHUNK, D_FF + jj * FF_CHUNK)]

    def conv(u, c0):
        cols = slice(c0, c0 + FF_CHUNK)
        c6 = carry_scr[6:7, cols]
        c7 = carry_scr[7:8, cols]
        r1 = pltpu.roll(u, 1, 0)
        r2 = pltpu.roll(u, 2, 0)
        p1 = jnp.concatenate([jnp.where(row8 == 0, c7, r1[0:8]), r1[8:]], axis=0)
        p2 = jnp.concatenate([jnp.where(row8 == 0, c6, jnp.where(row8 == 1, c7, r2[0:8])), r2[8:]], axis=0)
        carry_scr[:, cols] = u[tm - 8:tm, :]
        w = cw_ref[:, cols]
        return w[0:1] * p2 + w[1:2] * p1 + w[2:3] * u + cb_ref[:, cols]

    acc = jnp.zeros((tm, D_MODEL), F32)
    u_cur = up(0)
    acts = []
    for jj in range(n_chunks):
        u_next = up(jj + 1) if jj + 1 < n_chunks else None
        a = conv(u_cur[0], jj * FF_CHUNK)
        b = conv(u_cur[1], D_FF + jj * FF_CHUNK)
        acts.append((a * _sigmoid(a) * b).astype(BF16))
        if len(acts) == DOWN_GROUP or jj + 1 == n_chunks:
            first = jj + 1 - len(acts)
            acc = acc + _dot(jnp.concatenate(acts, axis=1), wdn_ref[first * FF_CHUNK:(jj + 1) * FF_CHUNK, :])
            acts = []
        u_cur = u_next
    o_ref[...] += acc


def mixer_mlp(h, s_out, r_out, d_out, prm, l, seq):
    n = h.shape[0]
    tm = TM_MLP

    def rows_in(width):
        return pl.BlockSpec((tm, width), lambda i: (i, 0))

    names = ("w_out", "g2", "w_up", "conv_w", "conv_b", "w_down")
    return pl.pallas_call(
        functools.partial(_mlp_kernel, tm=tm, tiles_per_seq=seq // tm),
        out_shape=jax.ShapeDtypeStruct((n, D_MODEL), F32),
        grid=(n // tm,),
        in_specs=[rows_in(D_MODEL), rows_in(SSM_WIDTH), rows_in(RET_WIDTH), rows_in(DIFF_WIDTH)]
        + [_layer_spec(prm[k], l, single=True) for k in names],
        out_specs=rows_in(D_MODEL),
        scratch_shapes=[pltpu.VMEM((SUBLANES, 2 * D_FF), F32)],
        compiler_params=_params(("arbitrary",), 56),
        name="mixer_mlp",
    )(h, s_out, r_out, d_out, *[prm[k] for k in names])


def _ple(h_ref, p_ref, g3_ref, wpg_ref, wpe_ref):
    h2 = h_ref[...]
    gate = _sigmoid(_dot(_rms(h2, g3_ref[...]).astype(BF16), wpg_ref[...].astype(BF16)))
    return h2 + gate * _dot(p_ref[...].astype(BF16), wpe_ref[...].astype(BF16))


def _ple_proj_kernel(h_ref, p_ref, g3_ref, wpg_ref, wpe_ref, g1_ref, w_ref,
                     ho_ref, u_ref, ret_ref, dk_ref, qt_ref, vt_ref):
    h3 = _ple(h_ref, p_ref, g3_ref, wpg_ref, wpe_ref)
    ho_ref[...] = h3
    hn = _rms(h3, g1_ref[...]).astype(BF16)
    _project(hn, w_ref, u_ref, ret_ref, dk_ref, qt_ref, vt_ref)


def _ple_final_kernel(h_ref, p_ref, g3_ref, wpg_ref, wpe_ref, gf_ref, o_ref):
    o_ref[...] = _rms(_ple(h_ref, p_ref, g3_ref, wpg_ref, wpe_ref), gf_ref[...])


def _ple_in_specs(n, tm, l, prm):
    return [pl.BlockSpec((tm, D_MODEL), lambda i: (i, 0)), pl.BlockSpec((tm, PLE_DIM), lambda i: (l * (n // tm) + i, 0)),
            _layer_spec(prm["g3"], l), _layer_spec(prm["w_pg"], l, single=True), _layer_spec(prm["w_pe"], l, single=True)]


def ple_proj(h, p_flat, prm, l):
    n = h.shape[0]
    tm = TM_PROJ
    shapes, specs = _proj_out(n, tm)
    return pl.pallas_call(
        _ple_proj_kernel,
        out_shape=(jax.ShapeDtypeStruct((n, D_MODEL), F32),) + shapes,
        grid=(n // tm,),
        in_specs=_ple_in_specs(n, tm, l, prm) + [_layer_spec(prm["g1"], l + 1), _layer_spec(prm["w_in"], l + 1, single=True)],
        out_specs=(pl.BlockSpec((tm, D_MODEL), lambda i: (i, 0)),) + specs,
        compiler_params=_params(("parallel",), 56),
        name="ple_proj",
    )(h, p_flat, prm["g3"], prm["w_pg"], prm["w_pe"], prm["g1"], prm["w_in"])


def ple_final(h, p_flat, prm, l, final_g):
    n = h.shape[0]
    tm = TM_PROJ
    return pl.pallas_call(
        _ple_final_kernel,
        out_shape=jax.ShapeDtypeStruct((n, D_MODEL), F32),
        grid=(n // tm,),
        in_specs=_ple_in_specs(n, tm, l, prm) + [pl.BlockSpec(final_g.shape, lambda i: (0, 0))],
        out_specs=pl.BlockSpec((tm, D_MODEL), lambda i: (i, 0)),
        compiler_params=_params(("parallel",), 40),
        name="ple_final",
    )(h, p_flat, prm["g3"], prm["w_pg"], prm["w_pe"], final_g)


def kernel(x, p, norm1_g, w_in, ssm_lam_re, ssm_lam_im, ssm_log_dt, ssm_b_re, ssm_b_im, ssm_c_re, ssm_c_im, ssm_d, ssm_w_glu, ssm_b_glu, ret_gn_g, diff_lq1, diff_lk1, diff_lq2, diff_lk2, diff_subln_g, w_out, norm2_g, w_up, conv_w, conv_b, w_down, norm3_g, w_pg, w_pe, final_g):
    nb, seq, _ = x.shape
    depth = w_in.shape[0]
    n = nb * seq
    assert seq % max(RET_CHUNK, ATT_TILE, S5_CHUNK, TM_MLP) == 0 and n % TM_PROJ == 0 and w_in.shape[2] == PROJ_WIDTH

    def row(a):
        return a[:, None, :]

    bmat, cmat, s5_a = s5_prep(ssm_lam_re, ssm_lam_im, ssm_log_dt, ssm_b_re, ssm_b_im, ssm_c_re, ssm_c_im)
    prm = {
        "w_in": w_in, "w_pg": w_pg, "w_pe": w_pe,
        "w_out": w_out.astype(BF16), "w_up": w_up.astype(BF16), "w_down": w_down.astype(BF16),
        "w_glu": ssm_w_glu.astype(BF16), "s5_b": bmat.astype(BF16), "s5_c": cmat.astype(BF16), "s5_a": s5_a,
        "g1": row(norm1_g), "g2": row(norm2_g), "g3": row(norm3_g), "ssm_d": row(ssm_d), "b_glu": row(ssm_b_glu),
        "gn": row(ret_gn_g), "subln": row(diff_subln_g), "conv_w": conv_w, "conv_b": row(conv_b),
        "lq1": row(diff_lq1), "lk1": row(diff_lk1), "lq2": row(diff_lq2), "lk2": row(diff_lk2),
    }
    p_flat = p.reshape(depth * n, PLE_DIM)

    h = x.reshape(n, D_MODEL)
    u, ret, dk, qt, vt = norm_proj(h, prm, 0)
    out = None
    for l in range(depth):
        lambda_init = 0.8 - 0.6 * math.exp(-0.3 * l)
        s_out = s5_mixer(u.reshape(nb, seq, SSM_WIDTH), prm, l)
        r_out = retention(ret, prm, l, nb, seq)
        d_out = diff_attention(qt, dk, vt, prm, l, lambda_init, nb, seq)
        h = mixer_mlp(h, s_out.reshape(n, SSM_WIDTH), r_out, d_out, prm, l, seq)
        if l + 1 < depth:
            h, u, ret, dk, qt, vt = ple_proj(h, p_flat, prm, l)
        else:
            out = ple_final(h, p_flat, prm, l, final_g[None, :])
    return out.reshape(nb, seq, D_MODEL)
```

```python
import functools
import math

import jax
import jax.numpy as jnp
import numpy as np
from jax import lax
from jax.experimental import pallas as pl
from jax.experimental.pallas import tpu as pltpu

F32 = jnp.float32
BF16 = jnp.bfloat16

D_MODEL = 1024
PLE_DIM = 256
SSM_WIDTH = 256
SSM_GROUP = 16
SSM_GROUPS = 16
SSM_STATE = 64
SSM_STATES = SSM_GROUPS * SSM_STATE
RET_HEADS = 6
RET_HEAD_DIM = 64
RET_WIDTH = RET_HEADS * RET_HEAD_DIM
DIFF_HEADS = 6
DIFF_QK_DIM = 32
DIFF_V_DIM = 64
DIFF_WIDTH = DIFF_HEADS * DIFF_V_DIM
D_FF = 2816
EPS = 1e-6
RET_LOG_GAMMA = np.log1p(-(2.0 ** (-5.0 - np.arange(RET_HEADS)))).astype(np.float32)
ALIBI_SLOPES = (2.0 ** (-8.0 * (np.arange(DIFF_HEADS) + 1) / DIFF_HEADS)).astype(np.float32)
PROJ_COLS = np.cumsum([0, SSM_WIDTH, 4 * RET_WIDTH, DIFF_WIDTH, DIFF_WIDTH, DIFF_WIDTH])
PROJ_WIDTH = int(PROJ_COLS[-1])

LANES = 128
SUBLANES = 8
VMEM_BYTES_V7X = 64 * 1024 * 1024
OUT_IN_HBM_VMEM_MIB = 56
NEG_BIG = -1e30
LOG2E = math.log2(math.e)

TM_PROJ = 1024
TM_MLP = 256
FF_CHUNK = 256
DOWN_GROUP = 11
S5_CHUNK = 128
S5_BLOCK = 16
RET_CHUNK = 256
ATT_TILE = 256
QV_TILE = 256


def _params(semantics, vmem_mib):
    assert vmem_mib * 1024 * 1024 < VMEM_BYTES_V7X
    return pltpu.CompilerParams(dimension_semantics=semantics, vmem_limit_bytes=vmem_mib * 1024 * 1024)


def _layer_spec(arr, l, single=False):
    nd = arr.ndim
    kw = {"pipeline_mode": pl.Buffered(1)} if single else {}
    return pl.BlockSpec((None,) + arr.shape[1:], lambda *_: (l,) + (0,) * (nd - 1), **kw)


def _rms(x, g):
    return x * lax.rsqrt(jnp.mean(x * x, axis=-1, keepdims=True) + EPS) * g


def _sigmoid(x):
    return 1.0 / (1.0 + jnp.exp2(x * -LOG2E))


def _dot(a, b):
    return jnp.dot(a, b, preferred_element_type=F32)


def _dot_nt(a, b):
    return lax.dot_general(a, b, (((1,), (1,)), ((), ())), preferred_element_type=F32)


def _project(hn, w_ref, u_ref, ret_ref, dk_ref, qt_ref, vt_ref):
    c = [int(v) for v in PROJ_COLS]

    def cols(a, b):
        return _dot(hn, w_ref[:, c[a]:c[b]].astype(BF16))

    u_ref[...] = cols(0, 1)
    ret_ref[...] = cols(1, 2).astype(BF16)
    qk = cols(2, 4)
    dk_ref[...] = qk[:, DIFF_WIDTH:].astype(BF16)
    q = qk[:, :DIFF_WIDTH] * (DIFF_QK_DIM ** -0.5 * LOG2E)
    v = cols(4, 5)
    for r in range(qt_ref.shape[0]):
        rows = slice(r * QV_TILE, (r + 1) * QV_TILE)
        qt_ref[r] = q[rows, :].T.astype(BF16)
        vt_ref[r] = v[rows, :].T.astype(BF16)


def _norm_proj_kernel(h_ref, g_ref, w_ref, u_ref, ret_ref, dk_ref, qt_ref, vt_ref):
    hn = _rms(h_ref[...], g_ref[...]).astype(BF16)
    _project(hn, w_ref, u_ref, ret_ref, dk_ref, qt_ref, vt_ref)


def _proj_out(n, tm):
    slab = jax.ShapeDtypeStruct((n // QV_TILE, DIFF_WIDTH, QV_TILE), BF16)
    shapes = (
        jax.ShapeDtypeStruct((n, SSM_WIDTH), F32),
        jax.ShapeDtypeStruct((n, 4 * RET_WIDTH), BF16),
        jax.ShapeDtypeStruct((n, DIFF_WIDTH), BF16),
        slab,
        slab,
    )
    slab_spec = pl.BlockSpec((tm // QV_TILE, DIFF_WIDTH, QV_TILE), lambda i: (i, 0, 0))
    specs = (
        pl.BlockSpec((tm, SSM_WIDTH), lambda i: (i, 0)),
        pl.BlockSpec((tm, 4 * RET_WIDTH), lambda i: (i, 0)),
        pl.BlockSpec((tm, DIFF_WIDTH), lambda i: (i, 0)),
        slab_spec,
        slab_spec,
    )
    return shapes, specs


def norm_proj(h, prm, l):
    n = h.shape[0]
    tm = TM_PROJ
    shapes, specs = _proj_out(n, tm)
    return pl.pallas_call(
        _norm_proj_kernel,
        out_shape=shapes,
        grid=(n // tm,),
        in_specs=[pl.BlockSpec((tm, D_MODEL), lambda i: (i, 0)), _layer_spec(prm["g1"], l),
                  _layer_spec(prm["w_in"], l, single=True)],
        out_specs=specs,
        compiler_params=_params(("parallel",), 48),
        name="norm_proj",
    )(h, prm["g1"], prm["w_in"])


def _s5_prep_kernel(lr_ref, li_ref, ldt_ref, brt_ref, bit_ref, crt_ref, cit_ref, b_ref, c_ref, a_ref):
    lr = lr_ref[0]
    li = li_ref[0]
    dt = jnp.exp(ldt_ref[0])
    mag = jnp.exp(lr * dt)
    ar = mag * jnp.cos(li * dt)
    ai = mag * jnp.sin(li * dt)
    den = lr * lr + li * li
    cr = ((ar - 1.0) * lr + ai * li) / den
    ci = (ai * lr - (ar - 1.0) * li) / den
    b_ref[...] = jnp.zeros_like(b_ref)
    c_ref[...] = jnp.zeros_like(c_ref)
    for g in range(SSM_GROUPS):
        rows = slice(g * SSM_GROUP, (g + 1) * SSM_GROUP)
        cols = slice(g * SSM_STATE, (g + 1) * SSM_STATE)
        cols_im = slice(SSM_STATES + g * SSM_STATE, SSM_STATES + (g + 1) * SSM_STATE)
        crg = cr[g:g + 1, :]
        cig = ci[g:g + 1, :]
        br = brt_ref[0, g]
        bi = bit_ref[0, g]
        b_ref[0, rows, cols] = crg * br - cig * bi
        b_ref[0, rows, cols_im] = crg * bi + cig * br
        c_ref[0, cols, rows] = crt_ref[0, g]
        c_ref[0, cols_im, rows] = -cit_ref[0, g]
        a_ref[0, 0:1, cols] = ar[g:g + 1, :]
        a_ref[0, 1:2, cols] = ai[g:g + 1, :]


def s5_prep(lam_re, lam_im, log_dt, b_re, b_im, c_re, c_im):
    depth = lam_re.shape[0]
    brt = jnp.transpose(b_re, (0, 1, 3, 2))
    bit = jnp.transpose(b_im, (0, 1, 3, 2))
    crt = jnp.transpose(c_re, (0, 1, 3, 2))
    cit = jnp.transpose(c_im, (0, 1, 3, 2))
    ldt = log_dt[..., None]

    def spec(a):
        nd = a.ndim
        return pl.BlockSpec((1,) + a.shape[1:], lambda l: (l,) + (0,) * (nd - 1))

    ins = (lam_re, lam_im, ldt, brt, bit, crt, cit)
    out_shape = (
        jax.ShapeDtypeStruct((depth, SSM_WIDTH, 2 * SSM_STATES), F32),
        jax.ShapeDtypeStruct((depth, 2 * SSM_STATES, SSM_WIDTH), F32),
        jax.ShapeDtypeStruct((depth, 2, SSM_STATES), F32),
    )
    return pl.pallas_call(
        _s5_prep_kernel,
        out_shape=out_shape,
        grid=(depth,),
        in_specs=[spec(a) for a in ins],
        out_specs=tuple(spec(o) for o in out_shape),
        compiler_params=_params(("parallel",), 32),
        name="s5_prep",
    )(*ins)


def _gelu_tanh(x):
    return 0.5 * x * (1.0 + jnp.tanh(math.sqrt(2.0 / math.pi) * (x + 0.044715 * (x * x * x))))


def _s5_kernel(un_ref, up_ref, bmat_ref, cmat_ref, a_ref, d_ref, wglu_ref, bglu_ref, o_ref,
               buf_scr, st_scr, *, tc):
    nb = un_ref.shape[0]
    g = pl.program_id(0)
    n_blk = tc // S5_BLOCK
    col_blk = 2 * SSM_STATES // n_blk

    @pl.when(g == 0)
    def _():
        buf_scr[...] = jnp.zeros_like(buf_scr)

    @pl.when(g <= 1)
    def _():
        st_scr[...] = jnp.zeros_like(st_scr)

    def stages(b_in, b_scan, b_out):
        ar = jnp.broadcast_to(a_ref[0:1, :], (nb, SSM_STATES))
        ai = jnp.broadcast_to(a_ref[1:2, :], (nb, SSM_STATES))
        xr = st_scr[:, 0:SSM_STATES]
        xi = st_scr[:, SSM_STATES:2 * SSM_STATES]
        u_next = jnp.swapaxes(un_ref[...], 0, 1).reshape(tc * nb, SSM_WIDTH).astype(BF16)
        y = d_ref[...] * jnp.swapaxes(up_ref[...], 0, 1).reshape(tc * nb, SSM_WIDTH)
        for j in range(n_blk):
            cols = slice(j * col_blk, (j + 1) * col_blk)
            y = y + _dot(buf_scr[b_out, :, cols].astype(BF16), cmat_ref[cols, :])
            for t in range(j * S5_BLOCK, (j + 1) * S5_BLOCK):
                r = t * nb
                nxr = ar * xr - ai * xi + buf_scr[b_scan, r:r + nb, 0:SSM_STATES]
                nxi = ar * xi + ai * xr + buf_scr[b_scan, r:r + nb, SSM_STATES:2 * SSM_STATES]
                buf_scr[b_scan, r:r + nb, 0:SSM_STATES] = nxr
                buf_scr[b_scan, r:r + nb, SSM_STATES:2 * SSM_STATES] = nxi
                xr, xi = nxr, nxi
            buf_scr[b_in, :, cols] = _dot(u_next, bmat_ref[:, cols])
        st_scr[:, 0:SSM_STATES] = xr
        st_scr[:, SSM_STATES:2 * SSM_STATES] = xi
        z = _gelu_tanh(y)
        out = z * _sigmoid(_dot(z.astype(BF16), wglu_ref[...]) + bglu_ref[...])
        o_ref[...] = jnp.swapaxes(out.reshape(tc, nb, SSM_WIDTH), 0, 1).astype(BF16)

    for r in range(3):
        @pl.when(g % 3 == r)
        def _(r=r):
            stages(r, (r + 2) % 3, (r + 1) % 3)


def s5_mixer(u, prm, l):
    nb, seq, _ = u.shape
    tc = S5_CHUNK
    nc = seq // tc
    assert nb == SUBLANES

    def chunk(fn):
        return pl.BlockSpec((nb, tc, SSM_WIDTH), lambda g: (0, fn(g), 0))

    names = ("s5_b", "s5_c", "s5_a", "ssm_d", "w_glu", "b_glu")
    return pl.pallas_call(
        functools.partial(_s5_kernel, tc=tc),
        out_shape=jax.ShapeDtypeStruct(u.shape, BF16),
        grid=(nc + 2,),
        in_specs=[chunk(lambda g: jnp.minimum(g, nc - 1)), chunk(lambda g: jnp.clip(g - 2, 0, nc - 1))]
        + [_layer_spec(prm[k], l) for k in names],
        out_specs=chunk(lambda g: jnp.clip(g - 2, 0, nc - 1)),
        scratch_shapes=[
            pltpu.VMEM((3, tc * nb, 2 * SSM_STATES), F32),
            pltpu.VMEM((nb, 2 * SSM_STATES), F32),
        ],
        compiler_params=_params(("arbitrary",), 48),
        name="s5_mixer",
    )(u, u, *[prm[k] for k in names])


def _ret_kernel(q_ref, k_ref, v_ref, g_ref, gn_ref, o_ref, dmat_scr, decq_scr, deck_scr, sdec_scr, st_scr, *, cr):
    pairs = RET_HEADS // 2
    hd = RET_HEAD_DIM
    scale = hd ** -0.5
    lane = lax.broadcasted_iota(jnp.int32, (1, LANES), 1)
    lo = lane < hd
    rows = lax.broadcasted_iota(jnp.int32, (LANES, LANES), 0) < hd
    cols = lax.broadcasted_iota(jnp.int32, (LANES, LANES), 1) < hd
    same = rows == cols

    @pl.when(pl.program_id(0) == 0)
    def _():
        t = lax.broadcasted_iota(jnp.int32, (cr, cr), 0)
        s = lax.broadcasted_iota(jnp.int32, (cr, cr), 1)
        dist = (t - s).astype(F32)
        pos = lax.broadcasted_iota(jnp.int32, (cr, LANES), 0).astype(F32)
        for h in range(RET_HEADS):
            lg = float(RET_LOG_GAMMA[h])
            dmat_scr[h // 2, :, (h % 2) * cr:(h % 2 + 1) * cr] = (
                jnp.where(dist >= 0, jnp.exp(jnp.maximum(dist, 0.0) * lg), 0.0) * scale)
        for j in range(pairs):
            lga, lgb = float(RET_LOG_GAMMA[2 * j]), float(RET_LOG_GAMMA[2 * j + 1])
            lg = jnp.where(lo, lga, lgb)
            decq_scr[j] = jnp.exp((pos + 1.0) * lg)
            deck_scr[j] = jnp.exp((cr - 1.0 - pos) * lg) * scale
            sdec_scr[j] = jnp.where(same, jnp.where(rows, math.exp(cr * lga), math.exp(cr * lgb)), 0.0)

    st_scr[...] = jnp.zeros_like(st_scr)

    avg2 =jnp.where(jnp.concatenate([same, same], axis=0), 1.0 / hd, 0.0).astype(BF16)
    blockmask = jnp.where(same, 1.0, 0.0).astype(F32)

    def group_mean(x):
        hi = x.astype(BF16)
        lo_part = (x - hi.astype(F32)).astype(BF16)
        return _dot(jnp.concatenate([hi, lo_part], axis=1), avg2)

    sls = [slice(j * LANES, (j + 1) * LANES) for j in range(pairs)]

    def chunk(c, carry):
        rows = pl.ds(pl.multiple_of(c * cr, cr), cr)
        scores = []
        for j in range(pairs):
            k = k_ref[rows, sls[j]]
            zero = jnp.zeros_like(k)
            k2 = jnp.concatenate([jnp.where(lo, k, zero), jnp.where(lo, zero, k)], axis=0)
            scores.append(_dot_nt(q_ref[rows, sls[j]], k2))
        probs = [(scores[j] * dmat_scr[j]).astype(BF16) for j in range(pairs)]
        outs = []
        for j in range(pairs):
            q = q_ref[rows, sls[j]]
            k = k_ref[rows, sls[j]]
            v = v_ref[rows, sls[j]]
            zero = jnp.zeros_like(v)
            v2 = jnp.concatenate([jnp.where(lo, v, zero), jnp.where(lo, zero, v)], axis=0)
            st = st_scr[j]
            cross = _dot((q.astype(F32) * decq_scr[j]).astype(BF16), st.astype(BF16))
            kdt = (k.astype(F32) * deck_scr[j]).T.astype(BF16)
            st_scr[j] = st * sdec_scr[j] + _dot(kdt, v) * blockmask
            outs.append(_dot(probs[j], v2) + cross)
        devs = [o - group_mean(o) for o in outs]
        variances = [group_mean(d * d) for d in devs]
        for j in range(pairs):
            on = devs[j] * lax.rsqrt(variances[j] + EPS) * gn_ref[:, sls[j]]
            gate = g_ref[rows, sls[j]].astype(F32)
            o_ref[rows, sls[j]] = (gate * _sigmoid(gate) * on).astype(BF16)
        return carry

    lax.fori_loop(0, q_ref.shape[0] // cr, chunk, 0)


def retention(ret, prm, l, nb, seq):
    n = ret.shape[0]
    cr = RET_CHUNK
    pairs = RET_HEADS // 2

    def part(col):
        return pl.BlockSpec((seq, RET_WIDTH), lambda b: (b, col))

    return pl.pallas_call(
        functools.partial(_ret_kernel, cr=cr),
        out_shape=jax.ShapeDtypeStruct((n, RET_WIDTH), BF16),
        grid=(nb,),
        in_specs=[part(0), part(1), part(2), part(3), _layer_spec(prm["gn"], l)],
        out_specs=pl.BlockSpec((seq, RET_WIDTH), lambda b: (b, 0)),
        scratch_shapes=[
            pltpu.VMEM((pairs, cr, 2 * cr), F32),
            pltpu.VMEM((pairs, cr, LANES), F32),
            pltpu.VMEM((pairs, cr, LANES), F32),
            pltpu.VMEM((pairs, LANES, LANES), F32),
            pltpu.VMEM((pairs, LANES, LANES), F32),
        ],
        compiler_params=_params(("arbitrary",), OUT_IN_HBM_VMEM_MIB),
        name="retention",
    )(ret, ret, ret, ret, prm["gn"])


def _split_bf16(x, parts=3):
    out = []
    rem = np.float32(x)
    for _ in range(parts):
        hi = np.float32(rem.astype(BF16))
        out.append(float(hi))
        rem = np.float32(rem - hi)
    return out


def _diff_kernel(lq1_ref, lk1_ref, lq2_ref, lk2_ref, qt_ref, k_ref, vt_ref, g_ref, o_ref,
                 acc_scr, m_scr, s_scr, mt_scr, qx_scr, *, tile, lambda_init):
    j = pl.program_id(1)
    nq = qt_ref.shape[0]
    hv = DIFF_V_DIM
    dq = DIFF_QK_DIM
    ones_rows = 16
    combos = 4
    n_split = 3

    lam = (jnp.exp(jnp.sum(lq1_ref[...] * lk1_ref[...], axis=-1, keepdims=True))
           - jnp.exp(jnp.sum(lq2_ref[...] * lk2_ref[...], axis=-1, keepdims=True)) + lambda_init)

    def pick(vals):
        return jnp.where(j == 0, vals[0], jnp.where(j == 1, vals[1], vals[2]))

    slope2 = [float(np.float32(ALIBI_SLOPES[h]) * np.float32(LOG2E)) for h in range(DIFF_HEADS)]
    parts = [_split_bf16(s, n_split) for s in slope2]
    slopes = [pick([slope2[2 * jj + hl] for jj in range(3)]) for hl in range(2)]
    pieces = [[pick([parts[2 * jj + hl][n] for jj in range(3)]) for n in range(n_split)] for hl in range(2)]

    s_loc = lax.broadcasted_iota(jnp.int32, (tile, tile), 0)
    t_loc = lax.broadcasted_iota(jnp.int32, (tile, tile), 1)
    causal = s_loc <= t_loc

    row_t = lax.broadcasted_iota(jnp.int32, (LANES, tile), 0)
    lane_t = lax.broadcasted_iota(jnp.int32, (tile, LANES), 1)
    qfeat = []
    for hl in range(2):
        f = jnp.zeros((LANES, tile), F32)
        for n in range(n_split):
            f = jnp.where(row_t == n, pieces[hl][n], f)
        qfeat += [f.astype(BF16)] * 2
    qfeat = jnp.concatenate(qfeat, axis=1)
    for i in range(nq):
        qt = qt_ref[i]

        def own_rows(c):
            pieces_c = [jnp.zeros((dq * c, tile), BF16), qt[dq * c:dq * (c + 1)],
                        jnp.zeros((LANES - dq * (c + 1), tile), BF16)]
            return jnp.concatenate([x for x in pieces_c if x.shape[0]], axis=0)

        qx_scr[i, 0:LANES, :] = jnp.concatenate([own_rows(c) for c in range(combos)], axis=1)
        qx_scr[i, LANES:2 * LANES, :] = qfeat
    pos = lax.broadcasted_iota(jnp.int32, (tile, LANES), 0).astype(F32)
    kfeat = jnp.where(lane_t < n_split, pos, 0.0).astype(BF16)
    ones = jnp.ones((ones_rows, tile), BF16)

    def score(i, t, buf, c):
        cols = slice(c * tile, (c + 1) * tile)
        k_ext = jnp.concatenate([k_ref[t * tile:(t + 1) * tile, :], kfeat], axis=1)
        s_c = _dot(k_ext, qx_scr[i, :, cols])
        if t == i:
            s_c = jnp.where(causal, s_c, NEG_BIG)
        s_scr[buf, :, cols] = s_c
        mt_scr[buf, c] = jnp.max(s_c, axis=0, keepdims=True)

    def softmax(i, t, buf, first, c):
        shift = slopes[c // 2] * float((t - i) * tile)
        m_tile = mt_scr[buf, c] + shift
        if first:
            m_new = m_tile
            alpha = None
        else:
            m_old = m_scr[i % 2, c]
            m_new = jnp.maximum(m_old, m_tile)
            alpha = jnp.exp2(m_old - m_new)
        m_scr[i % 2, c] = m_new
        return jnp.exp2(s_scr[buf, :, c * tile:(c + 1) * tile] - (m_new - shift)).astype(BF16), alpha

    def accumulate(i, t, hl, ps, alphas, first):
        vaug = jnp.concatenate([vt_ref[t, hl * hv:(hl + 1) * hv, :], ones], axis=0)
        pv = _dot(vaug, jnp.concatenate(ps, axis=1))
        if first:
            acc_scr[i % 2, hl] = pv
        else:
            acc_scr[i % 2, hl] = acc_scr[i % 2, hl] * jnp.concatenate(alphas, axis=1) + pv

    def finalize(i):
        outs = []
        for hl in range(2):
            a0 = acc_scr[i % 2, hl, :, 0:tile]
            a1 = acc_scr[i % 2, hl, :, tile:2 * tile]
            o = a0[0:hv] / a0[hv:hv + 1] - lam * (a1[0:hv] / a1[hv:hv + 1])
            outs.append(o * lax.rsqrt(jnp.mean(o * o, axis=0, keepdims=True) + EPS))
        ot = jnp.concatenate(outs, axis=0)
        o_ref[i * tile:(i + 1) * tile, :] = (ot.T * g_ref[...] * (1.0 - lambda_init)).astype(BF16)

    visits = [(i, t) for i in range(nq) for t in [i] + list(range(i))]
    for c in range(combos):
        score(*visits[0], 0, c)
    for n, (i, t) in enumerate(visits):
        buf = n % 2
        first = t == i
        ps, alphas = [], []
        for c in range(combos):
            if n + 1 < len(visits):
                score(*visits[n + 1], 1 - buf, c)
            p, alpha = softmax(i, t, buf, first, c)
            ps.append(p)
            alphas.append(alpha)
            if c % 2 == 1:
                accumulate(i, t, c // 2, ps[-2:], alphas[-2:], first)
        if n + 1 == len(visits) or visits[n + 1][0] != i:
            finalize(i)


def diff_attention(qt, dk, vt, prm, l, lambda_init, nb, seq):
    n = dk.shape[0]
    tile = ATT_TILE
    assert tile == QV_TILE
    nq = seq // tile
    pairs = DIFF_HEADS // 2
    lam_names = ("lq1", "lk1", "lq2", "lk2")
    slabs = pl.BlockSpec((nq, LANES, tile), lambda b, j: (b, j, 0))
    return pl.pallas_call(
        functools.partial(_diff_kernel, tile=tile, lambda_init=lambda_init),
        out_shape=jax.ShapeDtypeStruct((n, DIFF_WIDTH), BF16),
        grid=(nb, pairs),
        in_specs=[_layer_spec(prm[k], l) for k in lam_names] + [
            slabs,
            pl.BlockSpec((seq, LANES), lambda b, j: (b, j)),
            slabs,
            pl.BlockSpec((None, 1, LANES), lambda b, j: (l, 0, j)),
        ],
        out_specs=pl.BlockSpec((seq, LANES), lambda b, j: (b, j)),
        scratch_shapes=[
            pltpu.VMEM((2, 2, DIFF_V_DIM + 16, 2 * tile), F32),
            pltpu.VMEM((2, 4, 1, tile), F32),
            pltpu.VMEM((2, tile, 4 * tile), F32),
            pltpu.VMEM((2, 4, 1, tile), F32),
            pltpu.VMEM((nq, 2 * LANES, 4 * tile), BF16),
        ],
        compiler_params=_params(("parallel", "parallel"), OUT_IN_HBM_VMEM_MIB),
        name="diff_attention",
    )(*[prm[k] for k in lam_names], qt, dk, vt, prm["subln"])


def _mlp_kernel(h_ref, s_ref, r_ref, d_ref, wo_ref, g2_ref, wup_ref, cw_ref, cb_ref, wdn_ref, o_ref,
                carry_scr, *, tm, tiles_per_seq):
    @pl.when(pl.program_id(0) % tiles_per_seq == 0)
    def _():
        carry_scr[...] = jnp.zeros_like(carry_scr)

    mix = jnp.concatenate([s_ref[...], r_ref[...], d_ref[...]], axis=1)
    h1 = h_ref[...] + _dot(mix, wo_ref[...])
    o_ref[...] = h1
    hn = _rms(h1, g2_ref[...]).astype(BF16)
    row8 =lax.broadcasted_iota(jnp.int32, (SUBLANES, FF_CHUNK), 0)
    n_chunks = D_FF // FF_CHUNK

    def up(jj):
        return [_dot(hn, wup_ref[:, c0:c0 + FF_CHUNK]) for c0 in (jj * FF_CHUNK, D_FF + jj * FF_CHUNK)]

    def conv(u, c0):
        cols = slice(c0, c0 + FF_CHUNK)
        c6 = carry_scr[6:7, cols]
        c7 = carry_scr[7:8, cols]
        r1 = pltpu.roll(u, 1, 0)
        r2 = pltpu.roll(u, 2, 0)
        p1 = jnp.concatenate([jnp.where(row8 == 0, c7, r1[0:8]), r1[8:]], axis=0)
        p2 = jnp.concatenate([jnp.where(row8 == 0, c6, jnp.where(row8 == 1, c7, r2[0:8])), r2[8:]], axis=0)
        carry_scr[:, cols] = u[tm - 8:tm, :]
        w = cw_ref[:, cols]
        return w[0:1] * p2 + w[1:2] * p1 + w[2:3] * u + cb_ref[:, cols]

    acc = jnp.zeros((tm, D_MODEL), F32)
    u_cur = up(0)
    acts = []
    for jj in range(n_chunks):
        u_next = up(jj + 1) if jj + 1 < n_chunks else None
        a = conv(u_cur[0], jj * FF_CHUNK)
        b = conv(u_cur[1], D_FF + jj * FF_CHUNK)
        acts.append((a * _sigmoid(a) * b).astype(BF16))
        if len(acts) == DOWN_GROUP or jj + 1 == n_chunks:
            first = jj + 1 - len(acts)
            acc = acc + _dot(jnp.concatenate(acts, axis=1), wdn_ref[first * FF_CHUNK:(jj + 1) * FF_CHUNK, :])
            acts = []
        u_cur = u_next
    o_ref[...] += acc


def mixer_mlp(h, s_out, r_out, d_out, prm, l, seq):
    n = h.shape[0]
    tm = TM_MLP

    def rows_in(width):
        return pl.BlockSpec((tm, width), lambda i: (i, 0))

    names = ("w_out", "g2", "w_up", "conv_w", "conv_b", "w_down")
    return pl.pallas_call(
        functools.partial(_mlp_kernel, tm=tm, tiles_per_seq=seq // tm),
        out_shape=jax.ShapeDtypeStruct((n, D_MODEL), F32),
        grid=(n // tm,),
        in_specs=[rows_in(D_MODEL), rows_in(SSM_WIDTH), rows_in(RET_WIDTH), rows_in(DIFF_WIDTH)]
        + [_layer_spec(prm[k], l, single=True) for k in names],
        out_specs=rows_in(D_MODEL),
        scratch_shapes=[pltpu.VMEM((SUBLANES, 2 * D_FF), F32)],
        compiler_params=_params(("arbitrary",), 56),
        name="mixer_mlp",
    )(h, s_out, r_out, d_out, *[prm[k] for k in names])


def _ple(h_ref, p_ref, g3_ref, wpg_ref, wpe_ref):
    h2 = h_ref[...]
    gate = _sigmoid(_dot(_rms(h2, g3_ref[...]).astype(BF16), wpg_ref[...].astype(BF16)))
    return h2 + gate * _dot(p_ref[...].astype(BF16), wpe_ref[...].astype(BF16))


def _ple_proj_kernel(h_ref, p_ref, g3_ref, wpg_ref, wpe_ref, g1_ref, w_ref,
                     ho_ref, u_ref, ret_ref, dk_ref, qt_ref, vt_ref):
    h3 = _ple(h_ref, p_ref, g3_ref, wpg_ref, wpe_ref)
    ho_ref[...] = h3
    hn = _rms(h3, g1_ref[...]).astype(BF16)
    _project(hn, w_ref, u_ref, ret_ref, dk_ref, qt_ref, vt_ref)


def _ple_final_kernel(h_ref, p_ref, g3_ref, wpg_ref, wpe_ref, gf_ref, o_ref):
    o_ref[...] = _rms(_ple(h_ref, p_ref, g3_ref, wpg_ref, wpe_ref), gf_ref[...])


def _ple_in_specs(n, tm, l, prm):
    return [pl.BlockSpec((tm, D_MODEL), lambda i: (i, 0)), pl.BlockSpec((tm, PLE_DIM), lambda i: (l * (n // tm) + i, 0)),
            _layer_spec(prm["g3"], l), _layer_spec(prm["w_pg"], l, single=True), _layer_spec(prm["w_pe"], l, single=True)]


def ple_proj(h, p_flat, prm, l):
    n = h.shape[0]
    tm = TM_PROJ
    shapes, specs = _proj_out(n, tm)
    return pl.pallas_call(
        _ple_proj_kernel,
        out_shape=(jax.ShapeDtypeStruct((n, D_MODEL), F32),) + shapes,
        grid=(n // tm,),
        in_specs=_ple_in_specs(n, tm, l, prm) + [_layer_spec(prm["g1"], l + 1), _layer_spec(prm["w_in"], l + 1, single=True)],
        out_specs=(pl.BlockSpec((tm, D_MODEL), lambda i: (i, 0)),) + specs,
        compiler_params=_params(("parallel",), 56),
        name="ple_proj",
    )(h, p_flat, prm["g3"], prm["w_pg"], prm["w_pe"], prm["g1"], prm["w_in"])


def ple_final(h, p_flat, prm, l, final_g):
    n = h.shape[0]
    tm = TM_PROJ
    return pl.pallas_call(
        _ple_final_kernel,
        out_shape=jax.ShapeDtypeStruct((n, D_MODEL), F32),
        grid=(n // tm,),
        in_specs=_ple_in_specs(n, tm, l, prm) + [pl.BlockSpec(final_g.shape, lambda i: (0, 0))],
        out_specs=pl.BlockSpec((tm, D_MODEL), lambda i: (i, 0)),
        compiler_params=_params(("parallel",), 40),
        name="ple_final",
    )(h, p_flat, prm["g3"], prm["w_pg"], prm["w_pe"], final_g)


def kernel(x, p, norm1_g, w_in, ssm_lam_re, ssm_lam_im, ssm_log_dt, ssm_b_re, ssm_b_im, ssm_c_re, ssm_c_im, ssm_d, ssm_w_glu, ssm_b_glu, ret_gn_g, diff_lq1, diff_lk1, diff_lq2, diff_lk2, diff_subln_g, w_out, norm2_g, w_up, conv_w, conv_b, w_down, norm3_g, w_pg, w_pe, final_g):
    nb, seq, _ = x.shape
    depth = w_in.shape[0]
    n = nb * seq
    assert seq % max(RET_CHUNK, ATT_TILE, S5_CHUNK, TM_MLP) == 0 and n % TM_PROJ == 0 and w_in.shape[2] == PROJ_WIDTH

    def row(a):
        return a[:, None, :]

    bmat, cmat, s5_a = s5_prep(ssm_lam_re, ssm_lam_im, ssm_log_dt, ssm_b_re, ssm_b_im, ssm_c_re, ssm_c_im)
    prm = {
        "w_in": w_in, "w_pg": w_pg, "w_pe": w_pe,
        "w_out": w_out.astype(BF16), "w_up": w_up.astype(BF16), "w_down": w_down.astype(BF16),
        "w_glu": ssm_w_glu.astype(BF16), "s5_b": bmat.astype(BF16), "s5_c": cmat.astype(BF16), "s5_a": s5_a,
        "g1": row(norm1_g), "g2": row(norm2_g), "g3": row(norm3_g), "ssm_d": row(ssm_d), "b_glu": row(ssm_b_glu),
        "gn": row(ret_gn_g), "subln": row(diff_subln_g), "conv_w": conv_w, "conv_b": row(conv_b),
        "lq1": row(diff_lq1), "lk1": row(diff_lk1), "lq2": row(diff_lq2), "lk2": row(diff_lk2),
    }
    p_flat = p.reshape(depth * n, PLE_DIM)

    h = x.reshape(n, D_MODEL)
    u, ret, dk, qt, vt = norm_proj(h, prm, 0)
    out = None
    for l in range(depth):
        lambda_init = 0.8 - 0.6 * math.exp(-0.3 * l)
        s_out = s5_mixer(u.reshape(nb, seq, SSM_WIDTH), prm, l)
        r_out = retention(ret, prm, l, nb, seq)
        d_out = diff_attention(qt, dk, vt, prm, l, lambda_init, nb, seq)
        h = mixer_mlp(h, s_out.reshape(n, SSM_WIDTH), r_out, d_out, prm, l, seq)
        if l + 1 < depth:
            h, u, ret, dk, qt, vt = ple_proj(h, p_flat, prm, l)
        else:
            out = ple_final(h, p_flat, prm, l, final_g[None, :])
    return out.reshape(nb, seq, D_MODEL)
```

```python
import functools
import math

import jax
import jax.numpy as jnp
import numpy as np
from jax import lax
from jax.experimental import pallas as pl
from jax.experimental.pallas import tpu as pltpu

F32 = jnp.float32
BF16 = jnp.bfloat16

D_MODEL = 1024
PLE_DIM = 256
SSM_WIDTH = 256
SSM_GROUP = 16
SSM_GROUPS = 16
SSM_STATE = 64
SSM_STATES = SSM_GROUPS * SSM_STATE
RET_HEADS = 6
RET_HEAD_DIM = 64
RET_WIDTH = RET_HEADS * RET_HEAD_DIM
DIFF_HEADS = 6
DIFF_QK_DIM = 32
DIFF_V_DIM = 64
DIFF_WIDTH = DIFF_HEADS * DIFF_V_DIM
D_FF = 2816
EPS = 1e-6
RET_LOG_GAMMA = np.log1p(-(2.0 ** (-5.0 - np.arange(RET_HEADS)))).astype(np.float32)
ALIBI_SLOPES = (2.0 ** (-8.0 * (np.arange(DIFF_HEADS) + 1) / DIFF_HEADS)).astype(np.float32)
PROJ_COLS = np.cumsum([0, SSM_WIDTH, 4 * RET_WIDTH, DIFF_WIDTH, DIFF_WIDTH, DIFF_WIDTH])
PROJ_WIDTH = int(PROJ_COLS[-1])

LANES = 128
SUBLANES = 8
VMEM_BYTES_V7X = 64 * 1024 * 1024
OUT_IN_HBM_VMEM_MIB = 56
NEG_BIG = -1e30
LOG2E = math.log2(math.e)

TM_PROJ = 1024
TM_FINAL = 2048
TM_MLP = 256
FF_CHUNK = 256
DOWN_GROUP = 11
S5_CHUNK = 128
S5_BLOCK = 16
RET_CHUNK = 256
ATT_TILE = 256
QV_TILE = 256


def _params(semantics, vmem_mib):
    assert vmem_mib * 1024 * 1024 < VMEM_BYTES_V7X
    return pltpu.CompilerParams(dimension_semantics=semantics, vmem_limit_bytes=vmem_mib * 1024 * 1024)


def _layer_spec(arr, l, single=False):
    nd = arr.ndim
    kw = {"pipeline_mode": pl.Buffered(1)} if single else {}
    return pl.BlockSpec((None,) + arr.shape[1:], lambda *_: (l,) + (0,) * (nd - 1), **kw)


def _rms(x, g):
    return x * lax.rsqrt(jnp.mean(x * x, axis=-1, keepdims=True) + EPS) * g


def _sigmoid(x):
    return 1.0 / (1.0 + jnp.exp2(x * -LOG2E))


def _dot(a, b):
    return jnp.dot(a, b, preferred_element_type=F32)


def _dot_nt(a, b):
    return lax.dot_general(a, b, (((1,), (1,)), ((), ())), preferred_element_type=F32)


def _project(hn, w_ref, u_ref, ret_ref, dk_ref, qt_ref, vt_ref):
    c = [int(v) for v in PROJ_COLS]

    def cols(a, b):
        return _dot(hn, w_ref[:, c[a]:c[b]].astype(BF16))

    u_ref[...] = cols(0, 1)
    ret_ref[...] = cols(1, 2).astype(BF16)
    qk = cols(2, 4)
    dk_ref[...] = qk[:, DIFF_WIDTH:].astype(BF16)
    q = qk[:, :DIFF_WIDTH] * (DIFF_QK_DIM ** -0.5 * LOG2E)
    v = cols(4, 5)
    for r in range(qt_ref.shape[0]):
        rows = slice(r * QV_TILE, (r + 1) * QV_TILE)
        qt_ref[r] = q[rows, :].T.astype(BF16)
        vt_ref[r] = v[rows, :].T.astype(BF16)


def _norm_proj_kernel(h_ref, g_ref, w_ref, u_ref, ret_ref, dk_ref, qt_ref, vt_ref):
    hn = _rms(h_ref[...], g_ref[...]).astype(BF16)
    _project(hn, w_ref, u_ref, ret_ref, dk_ref, qt_ref, vt_ref)


def _proj_out(n, tm):
    slab = jax.ShapeDtypeStruct((n // QV_TILE, DIFF_WIDTH, QV_TILE), BF16)
    shapes = (
        jax.ShapeDtypeStruct((n, SSM_WIDTH), F32),
        jax.ShapeDtypeStruct((n, 4 * RET_WIDTH), BF16),
        jax.ShapeDtypeStruct((n, DIFF_WIDTH), BF16),
        slab,
        slab,
    )
    slab_spec = pl.BlockSpec((tm // QV_TILE, DIFF_WIDTH, QV_TILE), lambda i: (i, 0, 0))
    specs = (
        pl.BlockSpec((tm, SSM_WIDTH), lambda i: (i, 0)),
        pl.BlockSpec((tm, 4 * RET_WIDTH), lambda i: (i, 0)),
        pl.BlockSpec((tm, DIFF_WIDTH), lambda i: (i, 0)),
        slab_spec,
        slab_spec,
    )
    return shapes, specs


def norm_proj(h, prm, l):
    n = h.shape[0]
    tm = TM_PROJ
    shapes, specs = _proj_out(n, tm)
    return pl.pallas_call(
        _norm_proj_kernel,
        out_shape=shapes,
        grid=(n // tm,),
        in_specs=[pl.BlockSpec((tm, D_MODEL), lambda i: (i, 0)), _layer_spec(prm["g1"], l),
                  _layer_spec(prm["w_in"], l, single=True)],
        out_specs=specs,
        compiler_params=_params(("parallel",), 48),
        name="norm_proj",
    )(h, prm["g1"], prm["w_in"])


def _s5_prep_kernel(lr_ref, li_ref, ldt_ref, brt_ref, bit_ref, crt_ref, cit_ref, b_ref, c_ref, a_ref):
    lr = lr_ref[0]
    li = li_ref[0]
    dt = jnp.exp(ldt_ref[0])
    mag = jnp.exp(lr * dt)
    ar = mag * jnp.cos(li * dt)
    ai = mag * jnp.sin(li * dt)
    den = lr * lr + li * li
    cr = ((ar - 1.0) * lr + ai * li) / den
    ci = (ai * lr - (ar - 1.0) * li) / den
    b_ref[...] = jnp.zeros_like(b_ref)
    c_ref[...] = jnp.zeros_like(c_ref)
    for g in range(SSM_GROUPS):
        rows = slice(g * SSM_GROUP, (g + 1) * SSM_GROUP)
        cols = slice(g * SSM_STATE, (g + 1) * SSM_STATE)
        cols_im = slice(SSM_STATES + g * SSM_STATE, SSM_STATES + (g + 1) * SSM_STATE)
        crg = cr[g:g + 1, :]
        cig = ci[g:g + 1, :]
        br = brt_ref[0, g]
        bi = bit_ref[0, g]
        b_ref[0, rows, cols] = crg * br - cig * bi
        b_ref[0, rows, cols_im] = crg * bi + cig * br
        c_ref[0, cols, rows] = crt_ref[0, g]
        c_ref[0, cols_im, rows] = -cit_ref[0, g]
        a_ref[0, 0:1, cols] = ar[g:g + 1, :]
        a_ref[0, 1:2, cols] = ai[g:g + 1, :]


def s5_prep(lam_re, lam_im, log_dt, b_re, b_im, c_re, c_im):
    depth = lam_re.shape[0]
    brt = jnp.transpose(b_re, (0, 1, 3, 2))
    bit = jnp.transpose(b_im, (0, 1, 3, 2))
    crt = jnp.transpose(c_re, (0, 1, 3, 2))
    cit = jnp.transpose(c_im, (0, 1, 3, 2))
    ldt = log_dt[..., None]

    def spec(a):
        nd = a.ndim
        return pl.BlockSpec((1,) + a.shape[1:], lambda l: (l,) + (0,) * (nd - 1))

    ins = (lam_re, lam_im, ldt, brt, bit, crt, cit)
    out_shape = (
        jax.ShapeDtypeStruct((depth, SSM_WIDTH, 2 * SSM_STATES), F32),
        jax.ShapeDtypeStruct((depth, 2 * SSM_STATES, SSM_WIDTH), F32),
        jax.ShapeDtypeStruct((depth, 2, SSM_STATES), F32),
    )
    return pl.pallas_call(
        _s5_prep_kernel,
        out_shape=out_shape,
        grid=(depth,),
        in_specs=[spec(a) for a in ins],
        out_specs=tuple(spec(o) for o in out_shape),
        compiler_params=_params(("parallel",), 32),
        name="s5_prep",
    )(*ins)


def _gelu_tanh(x):
    return 0.5 * x * (1.0 + jnp.tanh(math.sqrt(2.0 / math.pi) * (x + 0.044715 * (x * x * x))))


def _s5_kernel(un_ref, up_ref, bmat_ref, cmat_ref, a_ref, d_ref, wglu_ref, bglu_ref, o_ref,
               buf_scr, st_scr, *, tc):
    nb = un_ref.shape[0]
    g = pl.program_id(0)
    n_blk = tc // S5_BLOCK
    col_blk = 2 * SSM_STATES // n_blk

    @pl.when(g == 0)
    def _():
        buf_scr[...] = jnp.zeros_like(buf_scr)

    @pl.when(g <= 1)
    def _():
        st_scr[...] = jnp.zeros_like(st_scr)

    def stages(b_in, b_scan, b_out):
        ar = jnp.broadcast_to(a_ref[0:1, :], (nb, SSM_STATES))
        ai = jnp.broadcast_to(a_ref[1:2, :], (nb, SSM_STATES))
        xr = st_scr[:, 0:SSM_STATES]
        xi = st_scr[:, SSM_STATES:2 * SSM_STATES]
        u_next = jnp.swapaxes(un_ref[...], 0, 1).reshape(tc * nb, SSM_WIDTH).astype(BF16)
        y = d_ref[...] * jnp.swapaxes(up_ref[...], 0, 1).reshape(tc * nb, SSM_WIDTH)
        for j in range(n_blk):
            cols = slice(j * col_blk, (j + 1) * col_blk)
            y = y + _dot(buf_scr[b_out, :, cols].astype(BF16), cmat_ref[cols, :])
            for t in range(j * S5_BLOCK, (j + 1) * S5_BLOCK):
                r = t * nb
                nxr = ar * xr - ai * xi + buf_scr[b_scan, r:r + nb, 0:SSM_STATES]
                nxi = ar * xi + ai * xr + buf_scr[b_scan, r:r + nb, SSM_STATES:2 * SSM_STATES]
                buf_scr[b_scan, r:r + nb, 0:SSM_STATES] = nxr
                buf_scr[b_scan, r:r + nb, SSM_STATES:2 * SSM_STATES] = nxi
                xr, xi = nxr, nxi
            buf_scr[b_in, :, cols] = _dot(u_next, bmat_ref[:, cols])
        st_scr[:, 0:SSM_STATES] = xr
        st_scr[:, SSM_STATES:2 * SSM_STATES] = xi
        z = _gelu_tanh(y)
        out = z * _sigmoid(_dot(z.astype(BF16), wglu_ref[...]) + bglu_ref[...])
        o_ref[...] = jnp.swapaxes(out.reshape(tc, nb, SSM_WIDTH), 0, 1).astype(BF16)

    for r in range(3):
        @pl.when(g % 3 == r)
        def _(r=r):
            stages(r, (r + 2) % 3, (r + 1) % 3)


def s5_mixer(u, prm, l):
    nb, seq, _ = u.shape
    tc = S5_CHUNK
    nc = seq // tc
    assert nb == SUBLANES

    def chunk(fn):
        return pl.BlockSpec((nb, tc, SSM_WIDTH), lambda g: (0, fn(g), 0))

    names = ("s5_b", "s5_c", "s5_a", "ssm_d", "w_glu", "b_glu")
    return pl.pallas_call(
        functools.partial(_s5_kernel, tc=tc),
        out_shape=jax.ShapeDtypeStruct(u.shape, BF16),
        grid=(nc + 2,),
        in_specs=[chunk(lambda g: jnp.minimum(g, nc - 1)), chunk(lambda g: jnp.clip(g - 2, 0, nc - 1))]
        + [_layer_spec(prm[k], l) for k in names],
        out_specs=chunk(lambda g: jnp.clip(g - 2, 0, nc - 1)),
        scratch_shapes=[
            pltpu.VMEM((3, tc * nb, 2 * SSM_STATES), F32),
            pltpu.VMEM((nb, 2 * SSM_STATES), F32),
        ],
        compiler_params=_params(("arbitrary",), 48),
        name="s5_mixer",
    )(u, u, *[prm[k] for k in names])


def _ret_kernel(q_ref, k_ref, v_ref, g_ref, gn_ref, o_ref, dmat_scr, decq_scr, deck_scr, sdec_scr, st_scr, *, cr):
    pairs = RET_HEADS // 2
    hd = RET_HEAD_DIM
    scale = hd ** -0.5
    lane = lax.broadcasted_iota(jnp.int32, (1, LANES), 1)
    lo = lane < hd
    rows = lax.broadcasted_iota(jnp.int32, (LANES, LANES), 0) < hd
    cols = lax.broadcasted_iota(jnp.int32, (LANES, LANES), 1) < hd
    same = rows == cols

    @pl.when(pl.program_id(0) == 0)
    def _():
        t = lax.broadcasted_iota(jnp.int32, (cr, cr), 0)
        s = lax.broadcasted_iota(jnp.int32, (cr, cr), 1)
        dist = (t - s).astype(F32)
        pos = lax.broadcasted_iota(jnp.int32, (cr, LANES), 0).astype(F32)
        for h in range(RET_HEADS):
            lg = float(RET_LOG_GAMMA[h])
            dmat_scr[h // 2, :, (h % 2) * cr:(h % 2 + 1) * cr] = (
                jnp.where(dist >= 0, jnp.exp(jnp.maximum(dist, 0.0) * lg), 0.0) * scale)
        for j in range(pairs):
            lga, lgb = float(RET_LOG_GAMMA[2 * j]), float(RET_LOG_GAMMA[2 * j + 1])
            lg = jnp.where(lo, lga, lgb)
            decq_scr[j] = jnp.exp((pos + 1.0) * lg)
            deck_scr[j] = jnp.exp((cr - 1.0 - pos) * lg) * scale
            sdec_scr[j] = jnp.where(same, jnp.where(rows, math.exp(cr * lga), math.exp(cr * lgb)), 0.0)

    st_scr[...] = jnp.zeros_like(st_scr)

    avg2 =jnp.where(jnp.concatenate([same, same], axis=0), 1.0 / hd, 0.0).astype(BF16)
    blockmask = jnp.where(same, 1.0, 0.0).astype(F32)

    def group_mean(x):
        hi = x.astype(BF16)
        lo_part = (x - hi.astype(F32)).astype(BF16)
        return _dot(jnp.concatenate([hi, lo_part], axis=1), avg2)

    sls = [slice(j * LANES, (j + 1) * LANES) for j in range(pairs)]

    def chunk(c, carry):
        rows = pl.ds(pl.multiple_of(c * cr, cr), cr)
        scores = []
        for j in range(pairs):
            k = k_ref[rows, sls[j]]
            zero = jnp.zeros_like(k)
            k2 = jnp.concatenate([jnp.where(lo, k, zero), jnp.where(lo, zero, k)], axis=0)
            scores.append(_dot_nt(q_ref[rows, sls[j]], k2))
        probs = [(scores[j] * dmat_scr[j]).astype(BF16) for j in range(pairs)]
        outs = []
        for j in range(pairs):
            q = q_ref[rows, sls[j]]
            k = k_ref[rows, sls[j]]
            v = v_ref[rows, sls[j]]
            zero = jnp.zeros_like(v)
            v2 = jnp.concatenate([jnp.where(lo, v, zero), jnp.where(lo, zero, v)], axis=0)
            st = st_scr[j]
            cross = _dot((q.astype(F32) * decq_scr[j]).astype(BF16), st.astype(BF16))
            kdt = (k.astype(F32) * deck_scr[j]).T.astype(BF16)
            st_scr[j] = st * sdec_scr[j] + _dot(kdt, v) * blockmask
            outs.append(_dot(probs[j], v2) + cross)
        devs = [o - group_mean(o) for o in outs]
        variances = [group_mean(d * d) for d in devs]
        for j in range(pairs):
            on = devs[j] * lax.rsqrt(variances[j] + EPS) * gn_ref[:, sls[j]]
            gate = g_ref[rows, sls[j]].astype(F32)
            o_ref[rows, sls[j]] = (gate * _sigmoid(gate) * on).astype(BF16)
        return carry

    lax.fori_loop(0, q_ref.shape[0] // cr, chunk, 0)


def retention(ret, prm, l, nb, seq):
    n = ret.shape[0]
    cr = RET_CHUNK
    pairs = RET_HEADS // 2

    def part(col):
        return pl.BlockSpec((seq, RET_WIDTH), lambda b: (b, col))

    return pl.pallas_call(
        functools.partial(_ret_kernel, cr=cr),
        out_shape=jax.ShapeDtypeStruct((n, RET_WIDTH), BF16),
        grid=(nb,),
        in_specs=[part(0), part(1), part(2), part(3), _layer_spec(prm["gn"], l)],
        out_specs=pl.BlockSpec((seq, RET_WIDTH), lambda b: (b, 0)),
        scratch_shapes=[
            pltpu.VMEM((pairs, cr, 2 * cr), F32),
            pltpu.VMEM((pairs, cr, LANES), F32),
            pltpu.VMEM((pairs, cr, LANES), F32),
            pltpu.VMEM((pairs, LANES, LANES), F32),
            pltpu.VMEM((pairs, LANES, LANES), F32),
        ],
        compiler_params=_params(("arbitrary",), OUT_IN_HBM_VMEM_MIB),
        name="retention",
    )(ret, ret, ret, ret, prm["gn"])


def _split_bf16(x, parts=3):
    out = []
    rem = np.float32(x)
    for _ in range(parts):
        hi = np.float32(rem.astype(BF16))
        out.append(float(hi))
        rem = np.float32(rem - hi)
    return out


def _diff_kernel(lq1_ref, lk1_ref, lq2_ref, lk2_ref, qt_ref, k_ref, vt_ref, g_ref, o_ref,
                 acc_scr, m_scr, s_scr, mt_scr, qx_scr, *, tile, lambda_init):
    j = pl.program_id(1)
    nq = qt_ref.shape[0]
    hv = DIFF_V_DIM
    dq = DIFF_QK_DIM
    ones_rows = 16
    combos = 4
    n_split = 3

    lam = (jnp.exp(jnp.sum(lq1_ref[...] * lk1_ref[...], axis=-1, keepdims=True))
           - jnp.exp(jnp.sum(lq2_ref[...] * lk2_ref[...], axis=-1, keepdims=True)) + lambda_init)

    def pick(vals):
        return jnp.where(j == 0, vals[0], jnp.where(j == 1, vals[1], vals[2]))

    slope2 = [float(np.float32(ALIBI_SLOPES[h]) * np.float32(LOG2E)) for h in range(DIFF_HEADS)]
    parts = [_split_bf16(s, n_split) for s in slope2]
    slopes = [pick([slope2[2 * jj + hl] for jj in range(3)]) for hl in range(2)]
    pieces = [[pick([parts[2 * jj + hl][n] for jj in range(3)]) for n in range(n_split)] for hl in range(2)]

    s_loc = lax.broadcasted_iota(jnp.int32, (tile, tile), 0)
    t_loc = lax.broadcasted_iota(jnp.int32, (tile, tile), 1)
    causal = s_loc <= t_loc

    row_t = lax.broadcasted_iota(jnp.int32, (LANES, tile), 0)
    lane_t = lax.broadcasted_iota(jnp.int32, (tile, LANES), 1)
    qfeat = []
    for hl in range(2):
        f = jnp.zeros((LANES, tile), F32)
        for n in range(n_split):
            f = jnp.where(row_t == n, pieces[hl][n], f)
        qfeat += [f.astype(BF16)] * 2
    qfeat = jnp.concatenate(qfeat, axis=1)
    for i in range(nq):
        qt = qt_ref[i]

        def own_rows(c):
            pieces_c = [jnp.zeros((dq * c, tile), BF16), qt[dq * c:dq * (c + 1)],
                        jnp.zeros((LANES - dq * (c + 1), tile), BF16)]
            return jnp.concatenate([x for x in pieces_c if x.shape[0]], axis=0)

        qx_scr[i, 0:LANES, :] = jnp.concatenate([own_rows(c) for c in range(combos)], axis=1)
        qx_scr[i, LANES:2 * LANES, :] = qfeat
    pos = lax.broadcasted_iota(jnp.int32, (tile, LANES), 0).astype(F32)
    kfeat = jnp.where(lane_t < n_split, pos, 0.0).astype(BF16)
    ones = jnp.ones((ones_rows, tile), BF16)

    def score(i, t, buf, c):
        cols = slice(c * tile, (c + 1) * tile)
        k_ext = jnp.concatenate([k_ref[t * tile:(t + 1) * tile, :], kfeat], axis=1)
        s_c = _dot(k_ext, qx_scr[i, :, cols])
        if t == i:
            s_c = jnp.where(causal, s_c, NEG_BIG)
        s_scr[buf, :, cols] = s_c
        mt_scr[buf, c] = jnp.max(s_c, axis=0, keepdims=True)

    def softmax(i, t, buf, first, c):
        shift = slopes[c // 2] * float((t - i) * tile)
        m_tile = mt_scr[buf, c] + shift
        if first:
            m_new = m_tile
            alpha = None
        else:
            m_old = m_scr[i % 2, c]
            m_new = jnp.maximum(m_old, m_tile)
            alpha = jnp.exp2(m_old - m_new)
        m_scr[i % 2, c] = m_new
        return jnp.exp2(s_scr[buf, :, c * tile:(c + 1) * tile] - (m_new - shift)).astype(BF16), alpha

    def accumulate(i, t, hl, ps, alphas, first):
        vaug = jnp.concatenate([vt_ref[t, hl * hv:(hl + 1) * hv, :], ones], axis=0)
        pv = _dot(vaug, jnp.concatenate(ps, axis=1))
        if first:
            acc_scr[i % 2, hl] = pv
        else:
            acc_scr[i % 2, hl] = acc_scr[i % 2, hl] * jnp.concatenate(alphas, axis=1) + pv

    def finalize(i):
        outs = []
        for hl in range(2):
            a0 = acc_scr[i % 2, hl, :, 0:tile]
            a1 = acc_scr[i % 2, hl, :, tile:2 * tile]
            o = a0[0:hv] / a0[hv:hv + 1] - lam * (a1[0:hv] / a1[hv:hv + 1])
            outs.append(o * lax.rsqrt(jnp.mean(o * o, axis=0, keepdims=True) + EPS))
        ot = jnp.concatenate(outs, axis=0)
        o_ref[i * tile:(i + 1) * tile, :] = (ot.T * g_ref[...] * (1.0 - lambda_init)).astype(BF16)

    visits = [(i, t) for i in range(nq) for t in [i] + list(range(i))]
    for c in range(combos):
        score(*visits[0], 0, c)
    for n, (i, t) in enumerate(visits):
        buf = n % 2
        first = t == i
        ps, alphas = [], []
        for c in range(combos):
            if n + 1 < len(visits):
                score(*visits[n + 1], 1 - buf, c)
            p, alpha = softmax(i, t, buf, first, c)
            ps.append(p)
            alphas.append(alpha)
            if c % 2 == 1:
                accumulate(i, t, c // 2, ps[-2:], alphas[-2:], first)
        if n + 1 == len(visits) or visits[n + 1][0] != i:
            finalize(i)


def diff_attention(qt, dk, vt, prm, l, lambda_init, nb, seq):
    n = dk.shape[0]
    tile = ATT_TILE
    assert tile == QV_TILE
    nq = seq // tile
    pairs = DIFF_HEADS // 2
    lam_names = ("lq1", "lk1", "lq2", "lk2")
    slabs = pl.BlockSpec((nq, LANES, tile), lambda b, j: (b, j, 0))
    return pl.pallas_call(
        functools.partial(_diff_kernel, tile=tile, lambda_init=lambda_init),
        out_shape=jax.ShapeDtypeStruct((n, DIFF_WIDTH), BF16),
        grid=(nb, pairs),
        in_specs=[_layer_spec(prm[k], l) for k in lam_names] + [
            slabs,
            pl.BlockSpec((seq, LANES), lambda b, j: (b, j)),
            slabs,
            pl.BlockSpec((None, 1, LANES), lambda b, j: (l, 0, j)),
        ],
        out_specs=pl.BlockSpec((seq, LANES), lambda b, j: (b, j)),
        scratch_shapes=[
            pltpu.VMEM((2, 2, DIFF_V_DIM + 16, 2 * tile), F32),
            pltpu.VMEM((2, 4, 1, tile), F32),
            pltpu.VMEM((2, tile, 4 * tile), F32),
            pltpu.VMEM((2, 4, 1, tile), F32),
            pltpu.VMEM((nq, 2 * LANES, 4 * tile), BF16),
        ],
        compiler_params=_params(("parallel", "parallel"), OUT_IN_HBM_VMEM_MIB),
        name="diff_attention",
    )(*[prm[k] for k in lam_names], qt, dk, vt, prm["subln"])


def _mlp_kernel(h_ref, s_ref, r_ref, d_ref, wo_ref, g2_ref, wup_ref, cw_ref, cb_ref, wdn_ref, o_ref,
                carry_scr, *, tm, tiles_per_seq):
    @pl.when(pl.program_id(0) % tiles_per_seq == 0)
    def _():
        carry_scr[...] = jnp.zeros_like(carry_scr)

    mix = jnp.concatenate([s_ref[...], r_ref[...], d_ref[...]], axis=1)
    h1 = h_ref[...] + _dot(mix, wo_ref[...])
    o_ref[...] = h1
    hn = _rms(h1, g2_ref[...]).astype(BF16)
    row8 =lax.broadcasted_iota(jnp.int32, (SUBLANES, FF_CHUNK), 0)
    n_chunks = D_FF // FF_CHUNK

    def up(jj):
        return [_dot(hn, wup_ref[:, c0:c0 + FF_CHUNK]) for c0 in (jj * FF_CHUNK, D_FF + jj * FF_CHUNK)]

    def conv(u, c0):
        cols = slice(c0, c0 + FF_CHUNK)
        c6 = carry_scr[6:7, cols]
        c7 = carry_scr[7:8, cols]
        r1 = pltpu.roll(u, 1, 0)
        r2 = pltpu.roll(u, 2, 0)
        p1 = jnp.concatenate([jnp.where(row8 == 0, c7, r1[0:8]), r1[8:]], axis=0)
        p2 = jnp.concatenate([jnp.where(row8 == 0, c6, jnp.where(row8 == 1, c7, r2[0:8])), r2[8:]], axis=0)
        carry_scr[:, cols] = u[tm - 8:tm, :]
        w = cw_ref[:, cols]
        return w[0:1] * p2 + w[1:2] * p1 + w[2:3] * u + cb_ref[:, cols]

    acc = jnp.zeros((tm, D_MODEL), F32)
    u_cur = up(0)
    acts = []
    for jj in range(n_chunks):
        u_next = up(jj + 1) if jj + 1 < n_chunks else None
        a = conv(u_cur[0], jj * FF_CHUNK)
        b = conv(u_cur[1], D_FF + jj * FF_CHUNK)
        acts.append((a * _sigmoid(a) * b).astype(BF16))
        if len(acts) == DOWN_GROUP or jj + 1 == n_chunks:
            first = jj + 1 - len(acts)
            acc = acc + _dot(jnp.concatenate(acts, axis=1), wdn_ref[first * FF_CHUNK:(jj + 1) * FF_CHUNK, :])
            acts = []
        u_cur = u_next
    o_ref[...] += acc


def mixer_mlp(h, s_out, r_out, d_out, prm, l, seq):
    n = h.shape[0]
    tm = TM_MLP

    def rows_in(width):
        return pl.BlockSpec((tm, width), lambda i: (i, 0))

    names = ("w_out", "g2", "w_up", "conv_w", "conv_b", "w_down")
    return pl.pallas_call(
        functools.partial(_mlp_kernel, tm=tm, tiles_per_seq=seq // tm),
        out_shape=jax.ShapeDtypeStruct((n, D_MODEL), F32),
        grid=(n // tm,),
        in_specs=[rows_in(D_MODEL), rows_in(SSM_WIDTH), rows_in(RET_WIDTH), rows_in(DIFF_WIDTH)]
        + [_layer_spec(prm[k], l, single=True) for k in names],
        out_specs=rows_in(D_MODEL),
        scratch_shapes=[pltpu.VMEM((SUBLANES, 2 * D_FF), F32)],
        compiler_params=_params(("arbitrary",), 56),
        name="mixer_mlp",
    )(h, s_out, r_out, d_out, *[prm[k] for k in names])


def _ple(h_ref, p_ref, g3_ref, wpg_ref, wpe_ref):
    h2 = h_ref[...]
    gate = _sigmoid(_dot(_rms(h2, g3_ref[...]).astype(BF16), wpg_ref[...].astype(BF16)))
    return h2 + gate * _dot(p_ref[...].astype(BF16), wpe_ref[...].astype(BF16))


def _ple_proj_kernel(h_ref, p_ref, g3_ref, wpg_ref, wpe_ref, g1_ref, w_ref,
                     ho_ref, u_ref, ret_ref, dk_ref, qt_ref, vt_ref):
    h3 = _ple(h_ref, p_ref, g3_ref, wpg_ref, wpe_ref)
    ho_ref[...] = h3
    hn = _rms(h3, g1_ref[...]).astype(BF16)
    _project(hn, w_ref, u_ref, ret_ref, dk_ref, qt_ref, vt_ref)


def _ple_final_kernel(h_ref, p_ref, g3_ref, wpg_ref, wpe_ref, gf_ref, o_ref):
    o_ref[...] = _rms(_ple(h_ref, p_ref, g3_ref, wpg_ref, wpe_ref), gf_ref[...])


def _ple_in_specs(n, tm, l, prm):
    return [pl.BlockSpec((tm, D_MODEL), lambda i: (i, 0)), pl.BlockSpec((tm, PLE_DIM), lambda i: (l * (n // tm) + i, 0)),
            _layer_spec(prm["g3"], l), _layer_spec(prm["w_pg"], l, single=True), _layer_spec(prm["w_pe"], l, single=True)]


def ple_proj(h, p_flat, prm, l):
    n = h.shape[0]
    tm = TM_PROJ
    shapes, specs = _proj_out(n, tm)
    return pl.pallas_call(
        _ple_proj_kernel,
        out_shape=(jax.ShapeDtypeStruct((n, D_MODEL), F32),) + shapes,
        grid=(n // tm,),
        in_specs=_ple_in_specs(n, tm, l, prm) + [_layer_spec(prm["g1"], l + 1), _layer_spec(prm["w_in"], l + 1, single=True)],
        out_specs=(pl.BlockSpec((tm, D_MODEL), lambda i: (i, 0)),) + specs,
        compiler_params=_params(("parallel",), 56),
        name="ple_proj",
    )(h, p_flat, prm["g3"], prm["w_pg"], prm["w_pe"], prm["g1"], prm["w_in"])


def ple_final(h, p_flat, prm, l, final_g):
    n = h.shape[0]
    tm = TM_FINAL
    assert n % tm == 0
    return pl.pallas_call(
        _ple_final_kernel,
        out_shape=jax.ShapeDtypeStruct((n, D_MODEL), F32),
        grid=(n // tm,),
        in_specs=_ple_in_specs(n, tm, l, prm) + [pl.BlockSpec(final_g.shape, lambda i: (0, 0))],
        out_specs=pl.BlockSpec((tm, D_MODEL), lambda i: (i, 0)),
        compiler_params=_params(("parallel",), 56),
        name="ple_final",
    )(h, p_flat, prm["g3"], prm["w_pg"], prm["w_pe"], final_g)


def kernel(x, p, norm1_g, w_in, ssm_lam_re, ssm_lam_im, ssm_log_dt, ssm_b_re, ssm_b_im, ssm_c_re, ssm_c_im, ssm_d, ssm_w_glu, ssm_b_glu, ret_gn_g, diff_lq1, diff_lk1, diff_lq2, diff_lk2, diff_subln_g, w_out, norm2_g, w_up, conv_w, conv_b, w_down, norm3_g, w_pg, w_pe, final_g):
    nb, seq, _ = x.shape
    depth = w_in.shape[0]
    n = nb * seq
    assert seq % max(RET_CHUNK, ATT_TILE, S5_CHUNK, TM_MLP) == 0 and n % TM_PROJ == 0 and w_in.shape[2] == PROJ_WIDTH

    def row(a):
        return a[:, None, :]

    bmat, cmat, s5_a = s5_prep(ssm_lam_re, ssm_lam_im, ssm_log_dt, ssm_b_re, ssm_b_im, ssm_c_re, ssm_c_im)
    prm = {
        "w_in": w_in, "w_pg": w_pg, "w_pe": w_pe,
        "w_out": w_out.astype(BF16), "w_up": w_up.astype(BF16), "w_down": w_down.astype(BF16),
        "w_glu": ssm_w_glu.astype(BF16), "s5_b": bmat.astype(BF16), "s5_c": cmat.astype(BF16), "s5_a": s5_a,
        "g1": row(norm1_g), "g2": row(norm2_g), "g3": row(norm3_g), "ssm_d": row(ssm_d), "b_glu": row(ssm_b_glu),
        "gn": row(ret_gn_g), "subln": row(diff_subln_g), "conv_w": conv_w, "conv_b": row(conv_b),
        "lq1": row(diff_lq1), "lk1": row(diff_lk1), "lq2": row(diff_lq2), "lk2": row(diff_lk2),
    }
    p_flat = p.reshape(depth * n, PLE_DIM)

    h = x.reshape(n, D_MODEL)
    u, ret, dk, qt, vt = norm_proj(h, prm, 0)
    out = None
    for l in range(depth):
        lambda_init = 0.8 - 0.6 * math.exp(-0.3 * l)
        s_out = s5_mixer(u.reshape(nb, seq, SSM_WIDTH), prm, l)
        r_out = retention(ret, prm, l, nb, seq)
        d_out = diff_attention(qt, dk, vt, prm, l, lambda_init, nb, seq)
        h = mixer_mlp(h, s_out.reshape(n, SSM_WIDTH), r_out, d_out, prm, l, seq)
        if l + 1 < depth:
            h, u, ret, dk, qt, vt = ple_proj(h, p_flat, prm, l)
        else:
            out = ple_final(h, p_flat, prm, l, final_g[None, :])
    return out.reshape(nb, seq, D_MODEL)
```
